```python
import jax, jax.numpy as jnp
from jax import lax
import numpy as np

D_MODEL = 1024
BATCH = 8
SEQ = 4096
DEPTH = 1
DEC_BATCH = 8
DEC_SEQ = 32
PAST_LEN = 4096

CHUNK = 64
HEAD_DIM = 64
A_HEADS = 8
A_WIDTH = A_HEADS * HEAD_DIM
Q_LORA = 256
KV_LORA = 128
NOPE_DIM = HEAD_DIM
ROPE_DIM = 32
ROPE_BASE = 10000.0
MLA_SCALE = (NOPE_DIM + ROPE_DIM) ** -0.5
B_HEADS = 8
B_WIDTH = B_HEADS * HEAD_DIM
BAND_CHUNKS = 8
MAX_REL = 128
N_REL = 2 * MAX_REL + 1
B_SCALE = HEAD_DIM ** -0.5
MIX_WIDTH = A_WIDTH + B_WIDTH
Q_BLOCK = 128
EPS = 1e-6
NEG = -1e30

OFF_CQ = 0
OFF_CKV = OFF_CQ + Q_LORA
OFF_KR = OFF_CKV + KV_LORA
OFF_GA = OFF_KR + ROPE_DIM
OFF_QB = OFF_GA + A_WIDTH
OFF_KB = OFF_QB + B_WIDTH
OFF_VB = OFF_KB + B_WIDTH
OFF_GB = OFF_VB + B_WIDTH
IN_WIDTH = OFF_GB + B_WIDTH

kernel_name = "hybrid_mla_chunkband_stream_step"


def rmsnorm(x, g):
    x32 = x.astype(jnp.float32)
    r = x32 * lax.rsqrt(jnp.mean(x32 * x32, axis=-1, keepdims=True) + EPS)
    return (r * g.astype(jnp.float32)).astype(x.dtype)


def rope(x, pos):
    half = ROPE_DIM // 2
    inv = ROPE_BASE ** (-jnp.arange(half, dtype=jnp.float32) / half)
    ang = pos.astype(jnp.float32)[:, None] * inv
    ang = ang.reshape(ang.shape[0], *([1] * (x.ndim - 3)), half)
    cos, sin = jnp.cos(ang), jnp.sin(ang)
    x1 = x[..., :half].astype(jnp.float32)
    x2 = x[..., half:].astype(jnp.float32)
    return jnp.concatenate([x1 * cos - x2 * sin, x1 * sin + x2 * cos], axis=-1).astype(x.dtype)


def mixer_inputs(xn, pos, w_in, g_cq, w_uq, g_ckv, w_uk):
    b, s, _ = xn.shape
    z = xn @ w_in
    c_q = rmsnorm(z[..., OFF_CQ:OFF_CKV], g_cq)
    q = jnp.einsum('bsr,rhd->bshd', c_q, w_uq)
    q_lat = jnp.einsum('bshd,chd->bshc', q[..., :NOPE_DIM], w_uk)
    q_pe = rope(q[..., NOPE_DIM:], pos)
    ckv = rmsnorm(z[..., OFF_CKV:OFF_KR], g_ckv)
    kpe = rope(z[..., OFF_KR:OFF_GA], pos)
    g_a = z[..., OFF_GA:OFF_QB]
    qb = z[..., OFF_QB:OFF_KB].reshape(b, s, B_HEADS, HEAD_DIM)
    kb = z[..., OFF_KB:OFF_VB].reshape(b, s, B_HEADS, HEAD_DIM)
    vb = z[..., OFF_VB:OFF_GB].reshape(b, s, B_HEADS, HEAD_DIM)
    g_b = z[..., OFF_GB:]
    return q_lat, q_pe, ckv, kpe, g_a, qb, kb, vb, g_b


def mla_attend(q_lat, q_pe, ckv, kpe, mask):
    sc = (jnp.einsum('bqhc,bkc->bhqk', q_lat, ckv)
          + jnp.einsum('bqhr,bkr->bhqk', q_pe, kpe)).astype(jnp.float32) * MLA_SCALE
    if mask is not None:
        sc = jnp.where(mask, sc, NEG)
    p = jax.nn.softmax(sc, axis=-1).astype(ckv.dtype)
    return jnp.einsum('bhqk,bkc->bqhc', p, ckv)


def mla_prompt(q_lat, q_pe, ckv, kpe):
    b, s = q_lat.shape[:2]
    nqb = s // Q_BLOCK
    kchunk = jnp.arange(s) // CHUNK

    def blk(args):
        ql, qp, i = args
        qchunk = (i * Q_BLOCK + jnp.arange(Q_BLOCK)) // CHUNK
        mask = kchunk[None, :] <= qchunk[:, None]
        return mla_attend(ql, qp, ckv, kpe, mask)

    qlb = q_lat.reshape(b, nqb, Q_BLOCK, *q_lat.shape[2:]).swapaxes(0, 1)
    qpb = q_pe.reshape(b, nqb, Q_BLOCK, *q_pe.shape[2:]).swapaxes(0, 1)
    out = lax.map(blk, (qlb, qpb, jnp.arange(nqb)))
    return out.swapaxes(0, 1).reshape(b, s, *out.shape[3:])


def rel_bias_lookup(rel_bias, dist):
    idx = jnp.clip(dist, -MAX_REL, MAX_REL) + MAX_REL
    return rel_bias[:, idx].astype(jnp.float32)


def band_prompt(q, k, v, rel_bias):
    b, s, h, d = q.shape
    nc = s // CHUNK
    w = (BAND_CHUNKS + 1) * CHUNK
    qc = q.reshape(b, nc, CHUNK, h, d)
    pad = ((0, 0), (BAND_CHUNKS, 0), (0, 0), (0, 0), (0, 0))
    kp = jnp.pad(k.reshape(b, nc, CHUNK, h, d), pad)
    vp = jnp.pad(v.reshape(b, nc, CHUNK, h, d), pad)
    kband = jnp.concatenate([kp[:, o:o + nc] for o in range(BAND_CHUNKS + 1)], axis=2)
    vband = jnp.concatenate([vp[:, o:o + nc] for o in range(BAND_CHUNKS + 1)], axis=2)
    sc = jnp.einsum('bnqhd,bnkhd->bnhqk', qc, kband).astype(jnp.float32) * B_SCALE
    a = jnp.arange(CHUNK)[:, None]
    kk = jnp.arange(w)[None, :]
    dist = (BAND_CHUNKS - kk // CHUNK) * CHUNK + a - kk % CHUNK
    sc = sc + rel_bias_lookup(rel_bias, dist)[None, None]
    valid = (jnp.arange(nc)[:, None] - BAND_CHUNKS + (jnp.arange(w) // CHUNK)[None, :]) >= 0
    sc = jnp.where(valid[None, :, None, None, :], sc, NEG)
    p = jax.nn.softmax(sc, axis=-1).astype(v.dtype)
    o = jnp.einsum('bnhqk,bnkhd->bnqhd', p, vband)
    return o.reshape(b, s, h, d)


def band_sample(q, k_new, v_new, kc, vc, rel_bias, past):
    t = q.shape[1]
    kl = kc.shape[1]
    k = jnp.concatenate([kc, k_new], axis=1)
    v = jnp.concatenate([vc, v_new], axis=1)
    qpos = past + jnp.arange(t)
    kpos = jnp.concatenate([past - kl + jnp.arange(kl), past + jnp.arange(t)])
    dist = qpos[:, None] - kpos[None, :]
    sc = jnp.einsum('bqhd,bkhd->bhqk', q, k).astype(jnp.float32) * B_SCALE
    sc = sc + rel_bias_lookup(rel_bias, dist)[None]
    p = jax.nn.softmax(sc, axis=-1).astype(v.dtype)
    return jnp.einsum('bhqk,bkhd->bqhd', p, v)


def merge_heads(o_lat, w_uv, g_a, o_b, g_b, w_out):
    b, s = o_b.shape[:2]
    o_a = jnp.einsum('bshc,chd->bshd', o_lat, w_uv).reshape(b, s, A_WIDTH)
    y = jnp.concatenate([o_a * jax.nn.silu(g_a), o_b.reshape(b, s, B_WIDTH) * jax.nn.silu(g_b)], axis=-1)
    return y @ w_out


def setup_inputs(seed: int = 0) -> dict:
    key = jax.random.key(seed)
    ks = jax.random.split(key, 16)
    kb_len = min(BAND_CHUNKS * CHUNK, PAST_LEN)
    f32 = jnp.float32
    n = lambda k, shape, s=1.0: (jax.random.normal(k, shape, f32) * s)
    return {
        "x_prompt": n(ks[0], (BATCH, SEQ, D_MODEL)),
        "x_sample": n(ks[1], (DEC_BATCH, DEC_SEQ, D_MODEL)),
        "cache_ckv": n(ks[2], (DEPTH, DEC_BATCH, PAST_LEN, KV_LORA)),
        "cache_kpe": n(ks[3], (DEPTH, DEC_BATCH, PAST_LEN, ROPE_DIM)),
        "cache_kb": n(ks[4], (DEPTH, DEC_BATCH, kb_len, B_HEADS, HEAD_DIM)),
        "cache_vb": n(ks[5], (DEPTH, DEC_BATCH, kb_len, B_HEADS, HEAD_DIM)),
        "w_in": n(ks[6], (DEPTH, D_MODEL, IN_WIDTH), D_MODEL ** -0.5),
        "g_mix": 1.0 + n(ks[7], (DEPTH, D_MODEL), 0.02),
        "g_cq": 1.0 + n(ks[8], (DEPTH, Q_LORA), 0.02),
        "w_uq": n(ks[9], (DEPTH, Q_LORA, A_HEADS, NOPE_DIM + ROPE_DIM), Q_LORA ** -0.5),
        "g_ckv": 1.0 + n(ks[10], (DEPTH, KV_LORA), 0.02),
        "w_uk": n(ks[11], (DEPTH, KV_LORA, A_HEADS, NOPE_DIM), KV_LORA ** -0.5),
        "w_uv": n(ks[12], (DEPTH, KV_LORA, A_HEADS, HEAD_DIM), KV_LORA ** -0.5),
        "rel_bias": n(ks[13], (DEPTH, B_HEADS, N_REL), 0.5),
        "w_out": n(ks[14], (DEPTH, MIX_WIDTH, D_MODEL), MIX_WIDTH ** -0.5),
        "g_final": 1.0 + n(ks[15], (D_MODEL,), 0.02),
    }


def reference(x_prompt, x_sample, cache_ckv, cache_kpe, cache_kb, cache_vb,
              w_in, g_mix, g_cq, w_uq, g_ckv, w_uk, w_uv, rel_bias, w_out, g_final):
    s = x_prompt.shape[1]
    t = x_sample.shape[1]
    past = cache_ckv.shape[2]
    kbp = min(BAND_CHUNKS * CHUNK, s)
    pos_p = jnp.arange(s)
    pos_s = past + jnp.arange(t)
    xp, xs = x_prompt, x_sample
    ckv_p, kpe_p, kb_p, vb_p = [], [], [], []
    ckv_s, kpe_s, kb_s, vb_s = [], [], [], []
    for l in range(DEPTH):
        q_lat, q_pe, ckv, kpe, g_a, qb, kb, vb, g_b = mixer_inputs(
            rmsnorm(xp, g_mix[l]), pos_p, w_in[l], g_cq[l], w_uq[l], g_ckv[l], w_uk[l])
        o_lat = mla_prompt(q_lat, q_pe, ckv, kpe)
        o_b = band_prompt(qb, kb, vb, rel_bias[l])
        xp = xp + merge_heads(o_lat, w_uv[l], g_a, o_b, g_b, w_out[l])
        ckv_p.append(ckv); kpe_p.append(kpe)
        kb_p.append(kb[:, s - kbp:]); vb_p.append(vb[:, s - kbp:])
        q_lat, q_pe, ckv, kpe, g_a, qb, kb, vb, g_b = mixer_inputs(
            rmsnorm(xs, g_mix[l]), pos_s, w_in[l], g_cq[l], w_uq[l], g_ckv[l], w_uk[l])
        ckv_all = jnp.concatenate([cache_ckv[l], ckv], axis=1)
        kpe_all = jnp.concatenate([cache_kpe[l], kpe], axis=1)
        o_lat = mla_attend(q_lat, q_pe, ckv_all, kpe_all, None)
        o_b = band_sample(qb, kb, vb, cache_kb[l], cache_vb[l], rel_bias[l], past)
        xs = xs + merge_heads(o_lat, w_uv[l], g_a, o_b, g_b, w_out[l])
        ckv_s.append(ckv); kpe_s.append(kpe); kb_s.append(kb); vb_s.append(vb)
    y_prompt = rmsnorm(xp, g_final)
    y_sample = rmsnorm(xs, g_final)
    return (y_prompt, y_sample,
            jnp.stack(ckv_p), jnp.stack(kpe_p), jnp.stack(kb_p), jnp.stack(vb_p),
            jnp.stack(ckv_s), jnp.stack(kpe_s), jnp.stack(kb_s), jnp.stack(vb_s))
```

```python
import functools

import jax
import jax.numpy as jnp
import numpy as np
from jax import lax
from jax.experimental import pallas as pl
from jax.experimental.pallas import tpu as pltpu

D_MODEL = 1024
CHUNK = 64
HEAD_DIM = 64
A_HEADS = 8
Q_LORA = 256
KV_LORA = 128
NOPE_DIM = 64
ROPE_DIM = 32
ROPE_BASE = 10000.0
MLA_SCALE = (NOPE_DIM + ROPE_DIM) ** -0.5
B_HEADS = 8
B_WIDTH = B_HEADS * HEAD_DIM
A_WIDTH = A_HEADS * HEAD_DIM
BAND_CHUNKS = 8
MAX_REL = 128
B_SCALE = HEAD_DIM ** -0.5
EPS = 1e-6
NEG = -1e30

OFF_CQ = 0
OFF_CKV = OFF_CQ + Q_LORA
OFF_KR = OFF_CKV + KV_LORA
OFF_GA = OFF_KR + ROPE_DIM
OFF_QB = OFF_GA + A_WIDTH
OFF_KB = OFF_QB + B_WIDTH
OFF_VB = OFF_KB + B_WIDTH
OFF_GB = OFF_VB + B_WIDTH

LANE = 128
N_PAIR = B_HEADS // 2
QK_WIDTH = 2 * LANE
BAND_KEYS = BAND_CHUNKS * CHUNK
SUB_Q = 2 * CHUNK
SUB_W = BAND_KEYS + SUB_Q
VMEM_LIMIT = 56 * 1024 * 1024

BF16 = jnp.bfloat16
F32 = jnp.float32


def _dot(a, b):
    return jnp.dot(a, b, preferred_element_type=F32)


def _dot_t(a, b):
    return lax.dot_general(a, b, (((1,), (1,)), ((), ())), preferred_element_type=F32)


def _silu(g):
    return g / (1.0 + jnp.exp(-g))


def _rms(x, g):
    return x * lax.rsqrt(jnp.mean(x * x, axis=-1, keepdims=True) + EPS) * g


def _proj_kernel(x_ref, cos_ref, sin_ref, gmix_ref, wcq_ref, wkv_ref, wga_ref, wqb_ref,
                 wkb_ref, wvb_ref, wgb_ref, gcq_ref, wqn_ref, wqpe_ref, wqpr_ref,
                 gckv_ref, wuk_ref,
                 qc_ref, kc_ref, ckv_ref, kpe_ref, ga_ref, gb_ref, qb_ref, kb_ref, vb_ref,
                 kbt_ref, vbt_ref, *, tail_blocks):
    i = pl.program_id(1)
    n_i = pl.num_programs(1)
    x = x_ref[0]
    xn = _rms(x, gmix_ref[...]).astype(BF16)
    cos = cos_ref[...]
    sin = sin_ref[...]

    cq = _rms(_dot(xn, wcq_ref[...]), gcq_ref[...]).astype(BF16)
    qn = _dot(cq, wqn_ref[...]).astype(BF16)
    qpe = _dot(cq, wqpe_ref[...])
    qpr = _dot(cq, wqpr_ref[...])
    for h in range(A_HEADS):
        p = h // 2
        q_lat = _dot(qn[:, p * LANE:(p + 1) * LANE], wuk_ref[h]) * MLA_SCALE
        sl = slice(h * LANE, (h + 1) * LANE)
        q_pe = (qpe[:, sl] * cos + qpr[:, sl] * sin) * MLA_SCALE
        qc_ref[0, h, :, 0:LANE] = q_lat.astype(BF16)
        qc_ref[0, h, :, LANE:QK_WIDTH] = q_pe.astype(BF16)

    zkv = _dot(xn, wkv_ref[...])
    ckv = _rms(zkv[:, 0:LANE], gckv_ref[...])
    kpe = zkv[:, LANE:2 * LANE] * cos + zkv[:, 2 * LANE:3 * LANE] * sin
    ckv_ref[0] = ckv
    kpe_ref[0] = kpe[:, 0:ROPE_DIM]
    kc_ref[0, :, 0:LANE] = ckv.astype(BF16)
    kc_ref[0, :, LANE:QK_WIDTH] = kpe.astype(BF16)

    ga_ref[0] = _dot(xn, wga_ref[...])
    gb_ref[0] = _dot(xn, wgb_ref[...])

    zq = _dot(xn, wqb_ref[...]) * B_SCALE
    zk = _dot(xn, wkb_ref[...])
    zv = _dot(xn, wvb_ref[...])
    for p in range(N_PAIR):
        sl = slice(p * LANE, (p + 1) * LANE)
        qb_ref[0, p] = zq[:, sl].astype(BF16)
        kb_ref[0, p] = zk[:, sl].astype(BF16)
        vb_ref[0, p] = zv[:, sl].astype(BF16)

    @pl.when(i >= n_i - tail_blocks)
    def _():
        kbt_ref[0] = zk
        vbt_ref[0] = zv


def _proj(x, cos, sin, w, tm):
    b, s, _ = x.shape
    n_i = s // tm
    kt = min(BAND_KEYS, s)
    tail_blocks = kt // tm
    assert s % tm == 0 and kt % tm == 0

    def full(a):
        nd = a.ndim
        return pl.BlockSpec(a.shape, lambda bi, i, _nd=nd: (0,) * _nd)

    weights = [w["g_mix"], w["w_cq"], w["w_kv"], w["w_ga"], w["w_qb"], w["w_kb"], w["w_vb"],
               w["w_gb"], w["g_cq"], w["w_qn"], w["w_qpe"], w["w_qpr"], w["g_ckv"], w["w_ukx"]]
    in_specs = [pl.BlockSpec((1, tm, D_MODEL), lambda bi, i: (bi, i, 0)),
                pl.BlockSpec((tm, LANE), lambda bi, i: (i, 0)),
                pl.BlockSpec((tm, LANE), lambda bi, i: (i, 0))] + [full(a) for a in weights]

    def tail_map(bi, i):
        return (bi, jnp.maximum(i - (n_i - tail_blocks), 0), 0)

    out_shape = [
        jax.ShapeDtypeStruct((b, A_HEADS, s, QK_WIDTH), BF16),
        jax.ShapeDtypeStruct((b, s, QK_WIDTH), BF16),
        jax.ShapeDtypeStruct((b, s, KV_LORA), F32),
        jax.ShapeDtypeStruct((b, s, ROPE_DIM), F32),
        jax.ShapeDtypeStruct((b, s, A_WIDTH), F32),
        jax.ShapeDtypeStruct((b, s, B_WIDTH), F32),
        jax.ShapeDtypeStruct((b, N_PAIR, s, LANE), BF16),
        jax.ShapeDtypeStruct((b, N_PAIR, s, LANE), BF16),
        jax.ShapeDtypeStruct((b, N_PAIR, s, LANE), BF16),
        jax.ShapeDtypeStruct((b, kt, B_WIDTH), F32),
        jax.ShapeDtypeStruct((b, kt, B_WIDTH), F32),
    ]
    out_specs = [
        pl.BlockSpec((1, A_HEADS, tm, QK_WIDTH), lambda bi, i: (bi, 0, i, 0)),
        pl.BlockSpec((1, tm, QK_WIDTH), lambda bi, i: (bi, i, 0)),
        pl.BlockSpec((1, tm, KV_LORA), lambda bi, i: (bi, i, 0)),
        pl.BlockSpec((1, tm, ROPE_DIM), lambda bi, i: (bi, i, 0)),
        pl.BlockSpec((1, tm, A_WIDTH), lambda bi, i: (bi, i, 0)),
        pl.BlockSpec((1, tm, B_WIDTH), lambda bi, i: (bi, i, 0)),
        pl.BlockSpec((1, N_PAIR, tm, LANE), lambda bi, i: (bi, 0, i, 0)),
        pl.BlockSpec((1, N_PAIR, tm, LANE), lambda bi, i: (bi, 0, i, 0)),
        pl.BlockSpec((1, N_PAIR, tm, LANE), lambda bi, i: (bi, 0, i, 0)),
        pl.BlockSpec((1, tm, B_WIDTH), tail_map),
        pl.BlockSpec((1, tm, B_WIDTH), tail_map),
    ]
    return pl.pallas_call(
        functools.partial(_proj_kernel, tail_blocks=tail_blocks),
        grid=(b, n_i),
        in_specs=in_specs,
        out_specs=out_specs,
        out_shape=out_shape,
        compiler_params=pltpu.CompilerParams(
            dimension_semantics=("arbitrary", "arbitrary"), vmem_limit_bytes=VMEM_LIMIT),
        name="proj",
    )(x, cos, sin, *weights)


def _mla_kernel(q_ref, k_ref, o_ref, m_ref, l_ref, acc_ref, *, tq):
    qi = pl.program_id(1)
    ki = pl.program_id(2)

    @pl.when(ki == 0)
    def _():
        m_ref[...] = jnp.full(m_ref.shape, NEG, F32)
        l_ref[...] = jnp.zeros(l_ref.shape, F32)
        acc_ref[...] = jnp.zeros(acc_ref.shape, F32)

    def step(masked):
        k = k_ref[0]
        v = k[:, 0:KV_LORA]
        if masked:
            row = lax.broadcasted_iota(jnp.int32, (tq, tq), 0) // CHUNK
            col = lax.broadcasted_iota(jnp.int32, (tq, tq), 1) // CHUNK
            visible = col <= row
        for h in range(A_HEADS):
            s = _dot_t(q_ref[0, h], k)
            if masked:
                s = jnp.where(visible, s, NEG)
            m_prev = m_ref[h]
            m_next = jnp.maximum(m_prev, jnp.max(s, axis=-1, keepdims=True))
            alpha = jnp.exp(m_prev - m_next)
            p = jnp.exp(s - m_next)
            l_ref[h] = alpha * l_ref[h] + jnp.sum(p, axis=-1, keepdims=True)
            m_ref[h] = m_next
            acc_ref[h] = alpha * acc_ref[h] + _dot(p.astype(BF16), v)

    @pl.when(ki < qi)
    def _():
        step(False)

    @pl.when(ki == qi)
    def _():
        step(True)
        for h in range(A_HEADS):
            o_ref[0, h] = (acc_ref[h] / l_ref[h]).astype(o_ref.dtype)


def _mla_prompt(qc, kc, tq):
    b, h, s, _ = qc.shape
    n = s // tq
    return pl.pallas_call(
        functools.partial(_mla_kernel, tq=tq),
        grid=(b, n, n),
        in_specs=[pl.BlockSpec((1, h, tq, QK_WIDTH), lambda bi, qi, ki: (bi, 0, qi, 0)),
                  pl.BlockSpec((1, tq, QK_WIDTH), lambda bi, qi, ki: (bi, jnp.minimum(ki, qi), 0))],
        out_specs=pl.BlockSpec((1, h, tq, KV_LORA), lambda bi, qi, ki: (bi, 0, qi, 0)),
        out_shape=jax.ShapeDtypeStruct((b, h, s, KV_LORA), BF16),
        scratch_shapes=[pltpu.VMEM((h, tq, 1), F32), pltpu.VMEM((h, tq, 1), F32),
                        pltpu.VMEM((h, tq, KV_LORA), F32)],
        compiler_params=pltpu.CompilerParams(
            dimension_semantics=("arbitrary", "arbitrary", "arbitrary"),
            vmem_limit_bytes=VMEM_LIMIT),
        name="mla_prompt",
    )(qc, kc)


def _mla_sample_kernel(q_ref, ckv_ref, kpe_ref, kn_ref, o_ref):
    hh, t, _ = q_ref.shape[1:]
    q = q_ref[0].reshape(hh * t, QK_WIDTH)
    ckv_c = ckv_ref[0].astype(BF16)
    kpe_c = kpe_ref[0].astype(BF16)
    kn = kn_ref[0]
    s_c = _dot_t(q[:, 0:KV_LORA], ckv_c) + _dot_t(q[:, KV_LORA:KV_LORA + ROPE_DIM], kpe_c)
    s_n = _dot_t(q, kn)
    m = jnp.maximum(jnp.max(s_c, axis=-1, keepdims=True), jnp.max(s_n, axis=-1, keepdims=True))
    p_c = jnp.exp(s_c - m)
    p_n = jnp.exp(s_n - m)
    l = jnp.sum(p_c, axis=-1, keepdims=True) + jnp.sum(p_n, axis=-1, keepdims=True)
    o = (_dot(p_c.astype(BF16), ckv_c) + _dot(p_n.astype(BF16), kn[:, 0:KV_LORA])) / l
    o_ref[0] = o.reshape(hh, t, KV_LORA).astype(o_ref.dtype)


def _mla_sample(qc, cache_ckv, cache_kpe, kc):
    b, h, t, _ = qc.shape
    past = cache_ckv.shape[1]
    return pl.pallas_call(
        _mla_sample_kernel,
        grid=(b,),
        in_specs=[pl.BlockSpec((1, h, t, QK_WIDTH), lambda bi: (bi, 0, 0, 0)),
                  pl.BlockSpec((1, past, KV_LORA), lambda bi: (bi, 0, 0)),
                  pl.BlockSpec((1, past, ROPE_DIM), lambda bi: (bi, 0, 0)),
                  pl.BlockSpec((1, t, QK_WIDTH), lambda bi: (bi, 0, 0))],
        out_specs=pl.BlockSpec((1, h, t, KV_LORA), lambda bi: (bi, 0, 0, 0)),
        out_shape=jax.ShapeDtypeStruct((b, h, t, KV_LORA), BF16),
        compiler_params=pltpu.CompilerParams(
            dimension_semantics=("arbitrary",), vmem_limit_bytes=VMEM_LIMIT),
        name="mla_sample",
    )(qc, cache_ckv, cache_kpe, kc)


def _pair_rows(q):
    lane = lax.broadcasted_iota(jnp.int32, q.shape, 1)
    zero = jnp.zeros_like(q)
    return jnp.concatenate([jnp.where(lane < HEAD_DIM, q, zero),
                            jnp.where(lane >= HEAD_DIM, q, zero)], axis=0)


def _pair_merge(o2, r):
    lane = lax.broadcasted_iota(jnp.int32, (r, LANE), 1)
    return jnp.where(lane < HEAD_DIM, o2[0:r], o2[r:2 * r])


def _band_kernel(q_ref, kp_ref, kcur_ref, vp_ref, vcur_ref, bias_ref, g_ref, y_ref,
                 k_scr, v_scr, *, tq):
    i = pl.program_id(1)
    k_scr[:, 0:tq, :] = kp_ref[0]
    k_scr[:, tq:2 * tq, :] = kcur_ref[0]
    v_scr[:, 0:tq, :] = vp_ref[0]
    v_scr[:, tq:2 * tq, :] = vcur_ref[0]
    has_prev = i > 0
    col = lax.broadcasted_iota(jnp.int32, (2 * SUB_Q, SUB_W), 1)
    for r in range(tq // SUB_Q):
        lo = tq - BAND_KEYS + r * SUB_Q
        in_cur = col >= (tq - lo)
        ok = jnp.logical_or(in_cur, has_prev)
        rows = slice(r * SUB_Q, (r + 1) * SUB_Q)
        for p in range(N_PAIR):
            q2 = _pair_rows(q_ref[0, p, rows, :])
            kw = k_scr[p, lo:lo + SUB_W, :]
            vw = v_scr[p, lo:lo + SUB_W, :]
            s = _dot_t(q2, kw) + bias_ref[p]
            s = jnp.where(ok, s, NEG)
            m = jnp.max(s, axis=-1, keepdims=True)
            e = jnp.exp(s - m)
            l = jnp.sum(e, axis=-1, keepdims=True)
            o2 = _dot(e.astype(BF16), vw) / l
            o = _pair_merge(o2, SUB_Q)
            g = g_ref[0, rows, p * LANE:(p + 1) * LANE]
            y_ref[0, rows, p * LANE:(p + 1) * LANE] = (o * _silu(g)).astype(y_ref.dtype)


def _band_prompt(qb, kb, vb, bias2, gb, tq):
    b, np_, s, _ = qb.shape
    n = s // tq
    assert tq >= BAND_KEYS and tq % SUB_Q == 0

    def cur(bi, i):
        return (bi, 0, i, 0)

    def prev(bi, i):
        return (bi, 0, jnp.maximum(i - 1, 0), 0)

    blk = (1, np_, tq, LANE)
    return pl.pallas_call(
        functools.partial(_band_kernel, tq=tq),
        grid=(b, n),
        in_specs=[pl.BlockSpec(blk, cur), pl.BlockSpec(blk, prev), pl.BlockSpec(blk, cur),
                  pl.BlockSpec(blk, prev), pl.BlockSpec(blk, cur),
                  pl.BlockSpec(bias2.shape, lambda bi, i: (0, 0, 0)),
                  pl.BlockSpec((1, tq, B_WIDTH), lambda bi, i: (bi, i, 0))],
        out_specs=pl.BlockSpec((1, tq, B_WIDTH), lambda bi, i: (bi, i, 0)),
        out_shape=jax.ShapeDtypeStruct((b, s, B_WIDTH), BF16),
        scratch_shapes=[pltpu.VMEM((np_, 2 * tq, LANE), BF16), pltpu.VMEM((np_, 2 * tq, LANE), BF16)],
        compiler_params=pltpu.CompilerParams(
            dimension_semantics=("arbitrary", "arbitrary"), vmem_limit_bytes=VMEM_LIMIT),
        name="band_prompt",
    )(qb, kb, kb, vb, vb, bias2, gb)


def _band_sample_kernel(q_ref, kn_ref, vn_ref, kc_ref, vc_ref, bias_ref, g_ref, y_ref):
    t = q_ref.shape[2]
    kl = kc_ref.shape[1]
    for p in range(N_PAIR):
        sl = slice(p * LANE, (p + 1) * LANE)
        q2 = _pair_rows(q_ref[0, p])
        kc = kc_ref[0, :, sl].astype(BF16)
        vc = vc_ref[0, :, sl].astype(BF16)
        kn = kn_ref[0, p]
        vn = vn_ref[0, p]
        s_c = _dot_t(q2, kc) + bias_ref[p, :, 0:kl]
        s_n = _dot_t(q2, kn) + bias_ref[p, :, kl:kl + t]
        m = jnp.maximum(jnp.max(s_c, axis=-1, keepdims=True), jnp.max(s_n, axis=-1, keepdims=True))
        e_c = jnp.exp(s_c - m)
        e_n = jnp.exp(s_n - m)
        l = jnp.sum(e_c, axis=-1, keepdims=True) + jnp.sum(e_n, axis=-1, keepdims=True)
        o2 = (_dot(e_c.astype(BF16), vc) + _dot(e_n.astype(BF16), vn)) / l
        o = _pair_merge(o2, t)
        y_ref[0, :, sl] = (o * _silu(g_ref[0, :, sl])).astype(y_ref.dtype)


def _band_sample(qb, kb, vb, cache_kb, cache_vb, bias2, gb):
    b, np_, t, _ = qb.shape
    kl = cache_kb.shape[1]
    new = pl.BlockSpec((1, np_, t, LANE), lambda bi: (bi, 0, 0, 0))
    cache = pl.BlockSpec((1, kl, B_WIDTH), lambda bi: (bi, 0, 0))
    return pl.pallas_call(
        _band_sample_kernel,
        grid=(b,),
        in_specs=[new, new, new, cache, cache,
                  pl.BlockSpec(bias2.shape, lambda bi: (0, 0, 0)),
                  pl.BlockSpec((1, t, B_WIDTH), lambda bi: (bi, 0, 0))],
        out_specs=pl.BlockSpec((1, t, B_WIDTH), lambda bi: (bi, 0, 0)),
        out_shape=jax.ShapeDtypeStruct((b, t, B_WIDTH), BF16),
        compiler_params=pltpu.CompilerParams(
            dimension_semantics=("arbitrary",), vmem_limit_bytes=VMEM_LIMIT),
        name="band_sample",
    )(qb, kb, vb, cache_kb, cache_vb, bias2, gb)


def _merge_kernel(ol_ref, ga_ref, yb_ref, x_ref, wuv_ref, woa_ref, wob_ref, gf_ref, y_ref):
    parts = []
    for p in range(A_HEADS // 2):
        parts.append(_dot(ol_ref[0, 2 * p], wuv_ref[2 * p]) + _dot(ol_ref[0, 2 * p + 1], wuv_ref[2 * p + 1]))
    o_a = jnp.concatenate(parts, axis=-1)
    ya = (o_a * _silu(ga_ref[0])).astype(BF16)
    acc = _dot(ya, woa_ref[...]) + _dot(yb_ref[0], wob_ref[...]) + x_ref[0]
    y_ref[0] = _rms(acc, gf_ref[...])


def _merge(o_lat, ga, yb, x, w, tm):
    b, s, _ = x.shape
    n_i = s // tm

    def full(a):
        nd = a.ndim
        return pl.BlockSpec(a.shape, lambda bi, i, _nd=nd: (0,) * _nd)

    weights = [w["w_uvx"], w["w_out_a"], w["w_out_b"], w["g_final"]]
    return pl.pallas_call(
        _merge_kernel,
        grid=(b, n_i),
        in_specs=[pl.BlockSpec((1, A_HEADS, tm, KV_LORA), lambda bi, i: (bi, 0, i, 0)),
                  pl.BlockSpec((1, tm, A_WIDTH), lambda bi, i: (bi, i, 0)),
                  pl.BlockSpec((1, tm, B_WIDTH), lambda bi, i: (bi, i, 0)),
                  pl.BlockSpec((1, tm, D_MODEL), lambda bi, i: (bi, i, 0))] + [full(a) for a in weights],
        out_specs=pl.BlockSpec((1, tm, D_MODEL), lambda bi, i: (bi, i, 0)),
        out_shape=jax.ShapeDtypeStruct((b, s, D_MODEL), F32),
        compiler_params=pltpu.CompilerParams(
            dimension_semantics=("arbitrary", "arbitrary"), vmem_limit_bytes=VMEM_LIMIT),
        name="merge",
    )(o_lat, ga, yb, x, *weights)


def _rope_tables(pos):
    half = ROPE_DIM // 2
    inv = ROPE_BASE ** (-jnp.arange(half, dtype=F32) / half)
    ang = pos.astype(F32)[:, None] * inv
    cos = jnp.concatenate([jnp.cos(ang), jnp.cos(ang)], axis=-1)
    sin = jnp.concatenate([jnp.sin(ang), jnp.sin(ang)], axis=-1)
    pad = ((0, 0), (0, LANE - ROPE_DIM))
    return jnp.pad(cos, pad), jnp.pad(sin, pad)


def _rot_cols(w):
    half = ROPE_DIM // 2
    return jnp.concatenate([-w[..., half:], w[..., :half]], axis=-1)


def _pad_lanes(w, width):
    return jnp.pad(w, [(0, 0)] * (w.ndim - 1) + [(0, width - w.shape[-1])])


def _prep_weights(w_in, g_mix, g_cq, w_uq, g_ckv, w_uk, w_uv, w_out, g_final):
    w_kr = w_in[:, OFF_KR:OFF_GA]
    w_kv = jnp.concatenate([w_in[:, OFF_CKV:OFF_KR], _pad_lanes(w_kr, LANE),
                            _pad_lanes(_rot_cols(w_kr), LANE)], axis=-1)
    w_pe = w_uq[:, :, NOPE_DIM:]
    zeros64 = jnp.zeros((A_HEADS, HEAD_DIM, KV_LORA), F32)
    uk_t = jnp.transpose(w_uk, (1, 2, 0))
    even = (jnp.arange(A_HEADS) % 2 == 0)[:, None, None]
    w_ukx = jnp.where(even, jnp.concatenate([uk_t, zeros64], axis=1),
                      jnp.concatenate([zeros64, uk_t], axis=1))
    uv = jnp.transpose(w_uv, (1, 0, 2))
    zeros_uv = jnp.zeros_like(uv)
    w_uvx = jnp.where(even, jnp.concatenate([uv, zeros_uv], axis=2),
                      jnp.concatenate([zeros_uv, uv], axis=2))
    return {
        "g_mix": g_mix.reshape(1, D_MODEL),
        "w_cq": w_in[:, OFF_CQ:OFF_CKV].astype(BF16),
        "w_kv": w_kv.astype(BF16),
        "w_ga": w_in[:, OFF_GA:OFF_QB].astype(BF16),
        "w_qb": w_in[:, OFF_QB:OFF_KB].astype(BF16),
        "w_kb": w_in[:, OFF_KB:OFF_VB].astype(BF16),
        "w_vb": w_in[:, OFF_VB:OFF_GB].astype(BF16),
        "w_gb": w_in[:, OFF_GB:].astype(BF16),
        "g_cq": g_cq.reshape(1, Q_LORA),
        "w_qn": w_uq[:, :, :NOPE_DIM].reshape(Q_LORA, A_WIDTH).astype(BF16),
        "w_qpe": _pad_lanes(w_pe, LANE).reshape(Q_LORA, A_HEADS * LANE).astype(BF16),
        "w_qpr": _pad_lanes(_rot_cols(w_pe), LANE).reshape(Q_LORA, A_HEADS * LANE).astype(BF16),
        "g_ckv": g_ckv.reshape(1, KV_LORA),
        "w_ukx": w_ukx.astype(BF16),
        "w_uvx": w_uvx.astype(BF16),
        "w_out_a": w_out[:A_WIDTH].astype(BF16),
        "w_out_b": w_out[A_WIDTH:].astype(BF16),
        "g_final": g_final.reshape(1, D_MODEL),
    }


def _pair_bias(bias):
    h, r, wd = bias.shape
    return bias.reshape(h // 2, 2 * r, wd)


def _band_prompt_bias(rel_bias):
    a = np.arange(SUB_Q)[:, None]
    j = np.arange(SUB_W)[None, :]
    dist = a + BAND_KEYS - j
    idx = np.clip(dist, -MAX_REL, MAX_REL) + MAX_REL
    back = (BAND_CHUNKS + a // CHUNK) - j // CHUNK
    visible = (back >= 0) & (back <= BAND_CHUNKS)
    bias = jnp.where(visible[None], rel_bias[:, idx].astype(F32), NEG)
    return _pair_bias(bias)


def _band_sample_bias(rel_bias, t, kl):
    qpos = np.arange(t)[:, None]
    kpos = np.concatenate([np.arange(kl) - kl, np.arange(t)])[None, :]
    idx = np.clip(qpos - kpos, -MAX_REL, MAX_REL) + MAX_REL
    return _pair_bias(rel_bias[:, idx].astype(F32))


def _layer(xp, xs, c_ckv, c_kpe, c_kb, c_vb, w, rel_bias, tabs_p, tabs_s):
    b, s, _ = xp.shape
    bs, t, _ = xs.shape
    kl = c_kb.shape[1]
    tm = 512

    qc, kc, ckv, kpe, ga, gb, qb, kb, vb, kbt, vbt = _proj(xp, *tabs_p, w, tm)
    o_lat = _mla_prompt(qc, kc, 512)
    yb = _band_prompt(qb, kb, vb, _band_prompt_bias(rel_bias), gb, 512)
    y_p = _merge(o_lat, ga, yb, xp, w, tm)

    qc_s, kc_s, ckv_s, kpe_s, ga_s, gb_s, qb_s, kb_s, vb_s, kbt_s, vbt_s = _proj(xs, *tabs_s, w, t)
    o_lat_s = _mla_sample(qc_s, c_ckv, c_kpe, kc_s)
    yb_s = _band_sample(qb_s, kb_s, vb_s, c_kb.reshape(bs, kl, B_WIDTH), c_vb.reshape(bs, kl, B_WIDTH),
                        _band_sample_bias(rel_bias, t, kl), gb_s)
    y_s = _merge(o_lat_s, ga_s, yb_s, xs, w, t)

    kt = kbt.shape[1]
    outs_p = (ckv, kpe, kbt.reshape(b, kt, B_HEADS, HEAD_DIM), vbt.reshape(b, kt, B_HEADS, HEAD_DIM))
    outs_s = (ckv_s, kpe_s, kbt_s.reshape(bs, t, B_HEADS, HEAD_DIM), vbt_s.reshape(bs, t, B_HEADS, HEAD_DIM))
    return y_p, y_s, outs_p, outs_s


def kernel(x_prompt, x_sample, cache_ckv, cache_kpe, cache_kb, cache_vb, w_in, g_mix, g_cq, w_uq,
           g_ckv, w_uk, w_uv, rel_bias, w_out, g_final):
    depth = w_in.shape[0]
    assert depth == 1, "final norm is fused into the single layer's merge kernel"
    s = x_prompt.shape[1]
    t = x_sample.shape[1]
    past = cache_ckv.shape[2]
    tabs_p = _rope_tables(jnp.arange(s))
    tabs_s = _rope_tables(past + jnp.arange(t))
    w = _prep_weights(w_in[0], g_mix[0], g_cq[0], w_uq[0], g_ckv[0], w_uk[0], w_uv[0], w_out[0], g_final)
    y_p, y_s, outs_p, outs_s = _layer(x_prompt, x_sample, cache_ckv[0], cache_kpe[0], cache_kb[0],
                                      cache_vb[0], w, rel_bias[0], tabs_p, tabs_s)
    return (y_p, y_s) + tuple(o[None] for o in outs_p) + tuple(o[None] for o in outs_s)
```

```python
import functools

import jax
import jax.numpy as jnp
import numpy as np
from jax import lax
from jax.experimental import pallas as pl
from jax.experimental.pallas import tpu as pltpu

D_MODEL = 1024
CHUNK = 64
HEAD_DIM = 64
A_HEADS = 8
Q_LORA = 256
KV_LORA = 128
NOPE_DIM = 64
ROPE_DIM = 32
ROPE_BASE = 10000.0
MLA_SCALE = (NOPE_DIM + ROPE_DIM) ** -0.5
LOG2E = 1.4426950408889634
Q_SCALE = MLA_SCALE * LOG2E
B_HEADS = 8
B_WIDTH = B_HEADS * HEAD_DIM
A_WIDTH = A_HEADS * HEAD_DIM
BAND_CHUNKS = 8
MAX_REL = 128
B_SCALE = HEAD_DIM ** -0.5
EPS = 1e-6
NEG = -1e30

OFF_CQ = 0
OFF_CKV = OFF_CQ + Q_LORA
OFF_KR = OFF_CKV + KV_LORA
OFF_GA = OFF_KR + ROPE_DIM
OFF_QB = OFF_GA + A_WIDTH
OFF_KB = OFF_QB + B_WIDTH
OFF_VB = OFF_KB + B_WIDTH
OFF_GB = OFF_VB + B_WIDTH

LANE = 128
N_PAIR = B_HEADS // 2
QK_WIDTH = 2 * LANE
BAND_KEYS = BAND_CHUNKS * CHUNK
SUB_Q = 2 * CHUNK
SUB_W = BAND_KEYS + SUB_Q
ROLL_W = SUB_W + LANE
VMEM_LIMIT = 56 * 1024 * 1024

BF16 = jnp.bfloat16
F32 = jnp.float32


def _dot(a, b):
    return jnp.dot(a, b, preferred_element_type=F32)


def _dot_t(a, b):
    return lax.dot_general(a, b, (((1,), (1,)), ((), ())), preferred_element_type=F32)


def _silu(g):
    return g / (1.0 + jnp.exp(-g))


def _rms(x, g):
    return x * lax.rsqrt(jnp.mean(x * x, axis=-1, keepdims=True) + EPS) * g


def _proj_kernel(x_ref, cos_ref, sin_ref, gmix_ref, wcq_ref, wkv_ref, wga_ref, wqb_ref,
                 wkb_ref, wvb_ref, wgb_ref, gcq_ref, wqn_ref, wqpe_ref, wqpr_ref,
                 gckv_ref, wuk_ref,
                 qc_ref, kc_ref, ckv_ref, kpe_ref, ga_ref, gb_ref, qb_ref, kb_ref, vb_ref,
                 kbt_ref, vbt_ref, *maybe_vt_ref, tail_blocks):
    i = pl.program_id(1)
    n_i = pl.num_programs(1)
    x = x_ref[0]
    xn = _rms(x, gmix_ref[...]).astype(BF16)
    cos = cos_ref[...]
    sin = sin_ref[...]

    cq = _rms(_dot(xn, wcq_ref[...]), gcq_ref[...]).astype(BF16)
    qn = _dot(cq, wqn_ref[...]).astype(BF16)
    qpe = _dot(cq, wqpe_ref[...])
    qpr = _dot(cq, wqpr_ref[...])
    for h in range(A_HEADS):
        p = h // 2
        q_lat = _dot(qn[:, p * LANE:(p + 1) * LANE], wuk_ref[h]) * Q_SCALE
        sl = slice(h * LANE, (h + 1) * LANE)
        q_pe = (qpe[:, sl] * cos + qpr[:, sl] * sin) * Q_SCALE
        qc_ref[0, h, :, 0:LANE] = q_lat.astype(BF16)
        qc_ref[0, h, :, LANE:QK_WIDTH] = q_pe.astype(BF16)

    zkv = _dot(xn, wkv_ref[...])
    ckv = _rms(zkv[:, 0:LANE], gckv_ref[...])
    kpe = zkv[:, LANE:2 * LANE] * cos + zkv[:, 2 * LANE:3 * LANE] * sin
    ckv_ref[0] = ckv
    kpe_ref[0] = kpe[:, 0:ROPE_DIM]
    kc_ref[0, :, 0:LANE] = ckv.astype(BF16)
    kc_ref[0, :, LANE:QK_WIDTH] = kpe.astype(BF16)
    for vt_ref in maybe_vt_ref:
        vt_ref[0] = ckv.T.astype(BF16)

    ga_ref[0] = _dot(xn, wga_ref[...])
    gb_ref[0] = _dot(xn, wgb_ref[...])

    zq = _dot(xn, wqb_ref[...]) * B_SCALE
    zk = _dot(xn, wkb_ref[...])
    zv = _dot(xn, wvb_ref[...])
    for p in range(N_PAIR):
        sl = slice(p * LANE, (p + 1) * LANE)
        qb_ref[0, p] = zq[:, sl].astype(BF16)
        kb_ref[0, p] = zk[:, sl].astype(BF16)
        vb_ref[0, p] = zv[:, sl].astype(BF16)

    @pl.when(i >= n_i - tail_blocks)
    def _():
        kbt_ref[0] = zk
        vbt_ref[0] = zv


def _proj(x, cos, sin, w, tm):
    b, s, _ = x.shape
    n_i = s // tm
    kt = min(BAND_KEYS, s)
    tail_blocks = kt // tm
    assert s % tm == 0 and kt % tm == 0

    def full(a):
        nd = a.ndim
        return pl.BlockSpec(a.shape, lambda bi, i, _nd=nd: (0,) * _nd)

    weights = [w["g_mix"], w["w_cq"], w["w_kv"], w["w_ga"], w["w_qb"], w["w_kb"], w["w_vb"],
               w["w_gb"], w["g_cq"], w["w_qn"], w["w_qpe"], w["w_qpr"], w["g_ckv"], w["w_ukx"]]
    in_specs = [pl.BlockSpec((1, tm, D_MODEL), lambda bi, i: (bi, i, 0)),
                pl.BlockSpec((tm, LANE), lambda bi, i: (i, 0)),
                pl.BlockSpec((tm, LANE), lambda bi, i: (i, 0))] + [full(a) for a in weights]

    def tail_map(bi, i):
        return (bi, jnp.maximum(i - (n_i - tail_blocks), 0), 0)

    out_shape = [
        jax.ShapeDtypeStruct((b, A_HEADS, s, QK_WIDTH), BF16),
        jax.ShapeDtypeStruct((b, s, QK_WIDTH), BF16),
        jax.ShapeDtypeStruct((b, s, KV_LORA), F32),
        jax.ShapeDtypeStruct((b, s, ROPE_DIM), F32),
        jax.ShapeDtypeStruct((b, s, A_WIDTH), F32),
        jax.ShapeDtypeStruct((b, s, B_WIDTH), F32),
        jax.ShapeDtypeStruct((b, N_PAIR, s, LANE), BF16),
        jax.ShapeDtypeStruct((b, N_PAIR, s, LANE), BF16),
        jax.ShapeDtypeStruct((b, N_PAIR, s, LANE), BF16),
        jax.ShapeDtypeStruct((b, kt, B_WIDTH), F32),
        jax.ShapeDtypeStruct((b, kt, B_WIDTH), F32),
    ]
    out_specs = [
        pl.BlockSpec((1, A_HEADS, tm, QK_WIDTH), lambda bi, i: (bi, 0, i, 0)),
        pl.BlockSpec((1, tm, QK_WIDTH), lambda bi, i: (bi, i, 0)),
        pl.BlockSpec((1, tm, KV_LORA), lambda bi, i: (bi, i, 0)),
        pl.BlockSpec((1, tm, ROPE_DIM), lambda bi, i: (bi, i, 0)),
        pl.BlockSpec((1, tm, A_WIDTH), lambda bi, i: (bi, i, 0)),
        pl.BlockSpec((1, tm, B_WIDTH), lambda bi, i: (bi, i, 0)),
        pl.BlockSpec((1, N_PAIR, tm, LANE), lambda bi, i: (bi, 0, i, 0)),
        pl.BlockSpec((1, N_PAIR, tm, LANE), lambda bi, i: (bi, 0, i, 0)),
        pl.BlockSpec((1, N_PAIR, tm, LANE), lambda bi, i: (bi, 0, i, 0)),
        pl.BlockSpec((1, tm, B_WIDTH), tail_map),
        pl.BlockSpec((1, tm, B_WIDTH), tail_map),
    ]
    if tm % LANE == 0:
        out_shape.append(jax.ShapeDtypeStruct((b, KV_LORA, s), BF16))
        out_specs.append(pl.BlockSpec((1, KV_LORA, tm), lambda bi, i: (bi, 0, i)))
    return pl.pallas_call(
        functools.partial(_proj_kernel, tail_blocks=tail_blocks),
        grid=(b, n_i),
        in_specs=in_specs,
        out_specs=out_specs,
        out_shape=out_shape,
        compiler_params=pltpu.CompilerParams(
            dimension_semantics=("arbitrary", "arbitrary"), vmem_limit_bytes=VMEM_LIMIT),
        name="proj",
    )(x, cos, sin, *weights)


def _mla_kernel(qi_ref, ki_ref, q_ref, k_ref, vt_ref, o_ref, m_ref, l_ref, acc_ref, *, tq):
    t = pl.program_id(1)
    qi = qi_ref[t]
    ki = ki_ref[t]

    @pl.when(ki == 0)
    def _():
        m_ref[...] = jnp.full(m_ref.shape, NEG, F32)
        l_ref[...] = jnp.zeros(l_ref.shape, F32)
        acc_ref[...] = jnp.zeros(acc_ref.shape, F32)

    def step(masked):
        k = k_ref[0]
        vt = vt_ref[0]
        if masked:
            key = lax.broadcasted_iota(jnp.int32, (tq, tq), 0) // CHUNK
            qry = lax.broadcasted_iota(jnp.int32, (tq, tq), 1) // CHUNK
            visible = key <= qry
        for h in range(A_HEADS):
            s = _dot_t(k, q_ref[0, h])
            if masked:
                s = jnp.where(visible, s, NEG)
            m_prev = m_ref[h]
            m_next = jnp.maximum(m_prev, jnp.max(s, axis=0, keepdims=True))
            alpha = jnp.exp2(m_prev - m_next)
            p = jnp.exp2(s - m_next)
            l_ref[h] = alpha * l_ref[h] + jnp.sum(p, axis=0, keepdims=True)
            m_ref[h] = m_next
            acc_ref[h] = alpha * acc_ref[h] + _dot(vt, p.astype(BF16))

    @pl.when(ki < qi)
    def _():
        step(False)

    @pl.when(ki == qi)
    def _():
        step(True)
        for h in range(A_HEADS):
            o_ref[0, h] = (acc_ref[h] / l_ref[h]).astype(o_ref.dtype)


def _mla_prompt(qc, kc, vt, tq):
    b, h, s, _ = qc.shape
    n = s // tq
    qi_tab = np.array([q for q in range(n) for _ in range(q + 1)], np.int32)
    ki_tab = np.array([k for q in range(n) for k in range(q + 1)], np.int32)
    grid_spec = pltpu.PrefetchScalarGridSpec(
        num_scalar_prefetch=2,
        grid=(b, len(qi_tab)),
        in_specs=[pl.BlockSpec((1, h, tq, QK_WIDTH), lambda bi, t, qt, kt: (bi, 0, qt[t], 0)),
                  pl.BlockSpec((1, tq, QK_WIDTH), lambda bi, t, qt, kt: (bi, kt[t], 0)),
                  pl.BlockSpec((1, KV_LORA, tq), lambda bi, t, qt, kt: (bi, 0, kt[t]))],
        out_specs=pl.BlockSpec((1, h, KV_LORA, tq), lambda bi, t, qt, kt: (bi, 0, 0, qt[t])),
        scratch_shapes=[pltpu.VMEM((h, 1, tq), F32), pltpu.VMEM((h, 1, tq), F32),
                        pltpu.VMEM((h, KV_LORA, tq), F32)],
    )
    return pl.pallas_call(
        functools.partial(_mla_kernel, tq=tq),
        grid_spec=grid_spec,
        out_shape=jax.ShapeDtypeStruct((b, h, KV_LORA, s), BF16),
        compiler_params=pltpu.CompilerParams(
            dimension_semantics=("arbitrary", "arbitrary"), vmem_limit_bytes=VMEM_LIMIT),
        name="mla_prompt",
    )(jnp.asarray(qi_tab), jnp.asarray(ki_tab), qc, kc, vt)


def _mla_sample_kernel(q_ref, ckv_ref, kpe_ref, kn_ref, o_ref):
    hh, t, _ = q_ref.shape[1:]
    q = q_ref[0].reshape(hh * t, QK_WIDTH)
    ckv_c = ckv_ref[0].astype(BF16)
    kpe_c = kpe_ref[0].astype(BF16)
    kn = kn_ref[0]
    s_c = _dot_t(q[:, 0:KV_LORA], ckv_c) + _dot_t(q[:, KV_LORA:KV_LORA + ROPE_DIM], kpe_c)
    s_n = _dot_t(q, kn)
    m = jnp.maximum(jnp.max(s_c, axis=-1, keepdims=True), jnp.max(s_n, axis=-1, keepdims=True))
    p_c = jnp.exp2(s_c - m)
    p_n = jnp.exp2(s_n - m)
    l = jnp.sum(p_c, axis=-1, keepdims=True) + jnp.sum(p_n, axis=-1, keepdims=True)
    o = (_dot(p_c.astype(BF16), ckv_c) + _dot(p_n.astype(BF16), kn[:, 0:KV_LORA])) / l
    o_ref[0] = o.reshape(hh, t, KV_LORA).astype(o_ref.dtype)


def _mla_sample(qc, cache_ckv, cache_kpe, kc):
    b, h, t, _ = qc.shape
    past = cache_ckv.shape[1]
    return pl.pallas_call(
        _mla_sample_kernel,
        grid=(b,),
        in_specs=[pl.BlockSpec((1, h, t, QK_WIDTH), lambda bi: (bi, 0, 0, 0)),
                  pl.BlockSpec((1, past, KV_LORA), lambda bi: (bi, 0, 0)),
                  pl.BlockSpec((1, past, ROPE_DIM), lambda bi: (bi, 0, 0)),
                  pl.BlockSpec((1, t, QK_WIDTH), lambda bi: (bi, 0, 0))],
        out_specs=pl.BlockSpec((1, h, t, KV_LORA), lambda bi: (bi, 0, 0, 0)),
        out_shape=jax.ShapeDtypeStruct((b, h, t, KV_LORA), BF16),
        compiler_params=pltpu.CompilerParams(
            dimension_semantics=("arbitrary",), vmem_limit_bytes=VMEM_LIMIT),
        name="mla_sample",
    )(qc, cache_ckv, cache_kpe, kc)


def _pair_rows(q):
    lane = lax.broadcasted_iota(jnp.int32, q.shape, 1)
    zero = jnp.zeros_like(q)
    return jnp.concatenate([jnp.where(lane < HEAD_DIM, q, zero),
                            jnp.where(lane >= HEAD_DIM, q, zero)], axis=0)


def _pair_merge(o2, r):
    lane = lax.broadcasted_iota(jnp.int32, (r, LANE), 1)
    return jnp.where(lane < HEAD_DIM, o2[0:r], o2[r:2 * r])


def _rel_bias_rows(rb_ref, h, rows):
    base = jnp.broadcast_to(rb_ref[h:h + 1, :], (rows, ROLL_W))
    return pltpu.roll(base, 0, 1, stride=1, stride_axis=0)


def _band_kernel(q_ref, kp_ref, kcur_ref, vp_ref, vcur_ref, rb_ref, g_ref, y_ref,
                 k_scr, v_scr, bias_ref, *, tq):
    i = pl.program_id(1)

    @pl.when(jnp.logical_and(pl.program_id(0) == 0, i == 0))
    def _():
        a = lax.broadcasted_iota(jnp.int32, (SUB_Q, SUB_W), 0) // CHUNK
        j = lax.broadcasted_iota(jnp.int32, (SUB_Q, SUB_W), 1) // CHUNK
        back = a + BAND_CHUNKS - j
        visible = jnp.logical_and(back >= 0, back <= BAND_CHUNKS)
        for h in range(B_HEADS):
            tile = _rel_bias_rows(rb_ref, h, SUB_Q)[:, 0:SUB_W]
            bias_ref[h // 2, (h % 2) * SUB_Q:(h % 2 + 1) * SUB_Q, :] = jnp.where(visible, tile, NEG)

    k_scr[:, 0:tq, :] = kp_ref[0]
    k_scr[:, tq:2 * tq, :] = kcur_ref[0]
    v_scr[:, 0:tq, :] = vp_ref[0]
    v_scr[:, tq:2 * tq, :] = vcur_ref[0]
    has_prev = i > 0
    col = lax.broadcasted_iota(jnp.int32, (2 * SUB_Q, SUB_W), 1)
    for r in range(tq // SUB_Q):
        lo = tq - BAND_KEYS + r * SUB_Q
        in_cur = col >= (tq - lo)
        ok = jnp.logical_or(in_cur, has_prev)
        rows = slice(r * SUB_Q, (r + 1) * SUB_Q)
        for p in range(N_PAIR):
            q2 = _pair_rows(q_ref[0, p, rows, :])
            kw = k_scr[p, lo:lo + SUB_W, :]
            vw = v_scr[p, lo:lo + SUB_W, :]
            s = _dot_t(q2, kw) + bias_ref[p]
            s = jnp.where(ok, s, NEG)
            m = jnp.max(s, axis=-1, keepdims=True)
            e = jnp.exp(s - m)
            l = jnp.sum(e, axis=-1, keepdims=True)
            o2 = _dot(e.astype(BF16), vw) / l
            o = _pair_merge(o2, SUB_Q)
            g = g_ref[0, rows, p * LANE:(p + 1) * LANE]
            y_ref[0, rows, p * LANE:(p + 1) * LANE] = (o * _silu(g)).astype(y_ref.dtype)


def _band_prompt(qb, kb, vb, rb, gb, tq):
    b, np_, s, _ = qb.shape
    n = s // tq
    assert tq >= BAND_KEYS and tq % SUB_Q == 0

    def cur(bi, i):
        return (bi, 0, i, 0)

    def prev(bi, i):
        return (bi, 0, jnp.maximum(i - 1, 0), 0)

    blk = (1, np_, tq, LANE)
    return pl.pallas_call(
        functools.partial(_band_kernel, tq=tq),
        grid=(b, n),
        in_specs=[pl.BlockSpec(blk, cur), pl.BlockSpec(blk, prev), pl.BlockSpec(blk, cur),
                  pl.BlockSpec(blk, prev), pl.BlockSpec(blk, cur),
                  pl.BlockSpec(rb.shape, lambda bi, i: (0, 0)),
                  pl.BlockSpec((1, tq, B_WIDTH), lambda bi, i: (bi, i, 0))],
        out_specs=pl.BlockSpec((1, tq, B_WIDTH), lambda bi, i: (bi, i, 0)),
        out_shape=jax.ShapeDtypeStruct((b, s, B_WIDTH), BF16),
        scratch_shapes=[pltpu.VMEM((np_, 2 * tq, LANE), BF16), pltpu.VMEM((np_, 2 * tq, LANE), BF16),
                        pltpu.VMEM((np_, 2 * SUB_Q, SUB_W), F32)],
        compiler_params=pltpu.CompilerParams(
            dimension_semantics=("arbitrary", "arbitrary"), vmem_limit_bytes=VMEM_LIMIT),
        name="band_prompt",
    )(qb, kb, kb, vb, vb, rb, gb)


def _band_sample_kernel(q_ref, kn_ref, vn_ref, kc_ref, vc_ref, rb_ref, g_ref, y_ref, bias_ref):
    t = q_ref.shape[2]
    kl = kc_ref.shape[1]

    @pl.when(pl.program_id(0) == 0)
    def _():
        for h in range(B_HEADS):
            bias_ref[h // 2, (h % 2) * t:(h % 2 + 1) * t, :] = _rel_bias_rows(rb_ref, h, t)

    for p in range(N_PAIR):
        sl = slice(p * LANE, (p + 1) * LANE)
        q2 = _pair_rows(q_ref[0, p])
        kc = kc_ref[0, :, sl].astype(BF16)
        vc = vc_ref[0, :, sl].astype(BF16)
        kn = kn_ref[0, p]
        vn = vn_ref[0, p]
        s_c = _dot_t(q2, kc) + bias_ref[p, :, 0:kl]
        s_n = _dot_t(q2, kn) + bias_ref[p, :, kl:kl + t]
        m = jnp.maximum(jnp.max(s_c, axis=-1, keepdims=True), jnp.max(s_n, axis=-1, keepdims=True))
        e_c = jnp.exp(s_c - m)
        e_n = jnp.exp(s_n - m)
        l = jnp.sum(e_c, axis=-1, keepdims=True) + jnp.sum(e_n, axis=-1, keepdims=True)
        o2 = (_dot(e_c.astype(BF16), vc) + _dot(e_n.astype(BF16), vn)) / l
        o = _pair_merge(o2, t)
        y_ref[0, :, sl] = (o * _silu(g_ref[0, :, sl])).astype(y_ref.dtype)


def _band_sample(qb, kb, vb, cache_kb, cache_vb, rb, gb):
    b, np_, t, _ = qb.shape
    kl = cache_kb.shape[1]
    assert kl == BAND_KEYS and kl + t <= ROLL_W - LANE
    new = pl.BlockSpec((1, np_, t, LANE), lambda bi: (bi, 0, 0, 0))
    cache = pl.BlockSpec((1, kl, B_WIDTH), lambda bi: (bi, 0, 0))
    return pl.pallas_call(
        _band_sample_kernel,
        grid=(b,),
        in_specs=[new, new, new, cache, cache,
                  pl.BlockSpec(rb.shape, lambda bi: (0, 0)),
                  pl.BlockSpec((1, t, B_WIDTH), lambda bi: (bi, 0, 0))],
        out_specs=pl.BlockSpec((1, t, B_WIDTH), lambda bi: (bi, 0, 0)),
        out_shape=jax.ShapeDtypeStruct((b, t, B_WIDTH), BF16),
        scratch_shapes=[pltpu.VMEM((np_, 2 * t, ROLL_W), F32)],
        compiler_params=pltpu.CompilerParams(
            dimension_semantics=("arbitrary",), vmem_limit_bytes=VMEM_LIMIT),
        name="band_sample",
    )(qb, kb, vb, cache_kb, cache_vb, rb, gb)


def _merge_kernel(ol_ref, ga_ref, yb_ref, x_ref, wuv_ref, woa_ref, wob_ref, gf_ref, y_ref, *,
                  latent_major):
    def up(h):
        if latent_major:
            return lax.dot_general(ol_ref[0, h], wuv_ref[h], (((0,), (0,)), ((), ())),
                                   preferred_element_type=F32)
        return _dot(ol_ref[0, h], wuv_ref[h])

    parts = []
    for p in range(A_HEADS // 2):
        parts.append(up(2 * p) + up(2 * p + 1))
    o_a = jnp.concatenate(parts, axis=-1)
    ya = (o_a * _silu(ga_ref[0])).astype(BF16)
    acc = _dot(ya, woa_ref[...]) + _dot(yb_ref[0], wob_ref[...]) + x_ref[0]
    y_ref[0] = _rms(acc, gf_ref[...])


def _merge(o_lat, ga, yb, x, w, tm, latent_major):
    b, s, _ = x.shape
    n_i = s // tm
    if latent_major:
        ol_spec = pl.BlockSpec((1, A_HEADS, KV_LORA, tm), lambda bi, i: (bi, 0, 0, i))
    else:
        ol_spec = pl.BlockSpec((1, A_HEADS, tm, KV_LORA), lambda bi, i: (bi, 0, i, 0))

    def full(a):
        nd = a.ndim
        return pl.BlockSpec(a.shape, lambda bi, i, _nd=nd: (0,) * _nd)

    weights = [w["w_uvx"], w["w_out_a"], w["w_out_b"], w["g_final"]]
    return pl.pallas_call(
        functools.partial(_merge_kernel, latent_major=latent_major),
        grid=(b, n_i),
        in_specs=[ol_spec,
                  pl.BlockSpec((1, tm, A_WIDTH), lambda bi, i: (bi, i, 0)),
                  pl.BlockSpec((1, tm, B_WIDTH), lambda bi, i: (bi, i, 0)),
                  pl.BlockSpec((1, tm, D_MODEL), lambda bi, i: (bi, i, 0))] + [full(a) for a in weights],
        out_specs=pl.BlockSpec((1, tm, D_MODEL), lambda bi, i: (bi, i, 0)),
        out_shape=jax.ShapeDtypeStruct((b, s, D_MODEL), F32),
        compiler_params=pltpu.CompilerParams(
            dimension_semantics=("arbitrary", "arbitrary"), vmem_limit_bytes=VMEM_LIMIT),
        name="merge",
    )(o_lat, ga, yb, x, *weights)


def _rope_tables(pos):
    half = ROPE_DIM // 2
    inv = ROPE_BASE ** (-jnp.arange(half, dtype=F32) / half)
    ang = pos.astype(F32)[:, None] * inv
    cos = jnp.concatenate([jnp.cos(ang), jnp.cos(ang)], axis=-1)
    sin = jnp.concatenate([jnp.sin(ang), jnp.sin(ang)], axis=-1)
    pad = ((0, 0), (0, LANE - ROPE_DIM))
    return jnp.pad(cos, pad), jnp.pad(sin, pad)


def _rot_cols(w):
    half = ROPE_DIM // 2
    return jnp.concatenate([-w[..., half:], w[..., :half]], axis=-1)


def _pad_lanes(w, width):
    return jnp.pad(w, [(0, 0)] * (w.ndim - 1) + [(0, width - w.shape[-1])])


def _prep_weights(w_in, g_mix, g_cq, w_uq, g_ckv, w_uk, w_uv, w_out, g_final):
    w_kr = w_in[:, OFF_KR:OFF_GA]
    w_kv = jnp.concatenate([w_in[:, OFF_CKV:OFF_KR], _pad_lanes(w_kr, LANE),
                            _pad_lanes(_rot_cols(w_kr), LANE)], axis=-1)
    w_pe = w_uq[:, :, NOPE_DIM:]
    zeros64 = jnp.zeros((A_HEADS, HEAD_DIM, KV_LORA), F32)
    uk_t = jnp.transpose(w_uk, (1, 2, 0))
    even = (jnp.arange(A_HEADS) % 2 == 0)[:, None, None]
    w_ukx = jnp.where(even, jnp.concatenate([uk_t, zeros64], axis=1),
                      jnp.concatenate([zeros64, uk_t], axis=1))
    uv = jnp.transpose(w_uv, (1, 0, 2))
    zeros_uv = jnp.zeros_like(uv)
    w_uvx = jnp.where(even, jnp.concatenate([uv, zeros_uv], axis=2),
                      jnp.concatenate([zeros_uv, uv], axis=2))
    return {
        "g_mix": g_mix.reshape(1, D_MODEL),
        "w_cq": w_in[:, OFF_CQ:OFF_CKV].astype(BF16),
        "w_kv": w_kv.astype(BF16),
        "w_ga": w_in[:, OFF_GA:OFF_QB].astype(BF16),
        "w_qb": w_in[:, OFF_QB:OFF_KB].astype(BF16),
        "w_kb": w_in[:, OFF_KB:OFF_VB].astype(BF16),
        "w_vb": w_in[:, OFF_VB:OFF_GB].astype(BF16),
        "w_gb": w_in[:, OFF_GB:].astype(BF16),
        "g_cq": g_cq.reshape(1, Q_LORA),
        "w_qn": w_uq[:, :, :NOPE_DIM].reshape(Q_LORA, A_WIDTH).astype(BF16),
        "w_qpe": _pad_lanes(w_pe, LANE).reshape(Q_LORA, A_HEADS * LANE).astype(BF16),
        "w_qpr": _pad_lanes(_rot_cols(w_pe), LANE).reshape(Q_LORA, A_HEADS * LANE).astype(BF16),
        "g_ckv": g_ckv.reshape(1, KV_LORA),
        "w_ukx": w_ukx.astype(BF16),
        "w_uvx": w_uvx.astype(BF16),
        "w_out_a": w_out[:A_WIDTH].astype(BF16),
        "w_out_b": w_out[A_WIDTH:].astype(BF16),
        "g_final": g_final.reshape(1, D_MODEL),
    }


def _rel_bias_vector(rel_bias):
    far = rel_bias[:, 2 * MAX_REL:]
    n_far = BAND_KEYS - MAX_REL + 1
    near = rel_bias[:, 2 * MAX_REL - 1:0:-1]
    n_tail = ROLL_W - n_far - near.shape[1]
    h = rel_bias.shape[0]
    return jnp.concatenate([jnp.broadcast_to(far, (h, n_far)), near,
                            jnp.broadcast_to(far, (h, n_tail))], axis=1).astype(F32)


def _layer(xp, xs, c_ckv, c_kpe, c_kb, c_vb, w, rel_bias, tabs_p, tabs_s):
    b, s, _ = xp.shape
    bs, t, _ = xs.shape
    kl = c_kb.shape[1]
    tm = 512

    qc, kc, ckv, kpe, ga, gb, qb, kb, vb, kbt, vbt, vt = _proj(xp, *tabs_p, w, tm)
    o_lat = _mla_prompt(qc, kc, vt, 512)
    rb = _rel_bias_vector(rel_bias)
    yb = _band_prompt(qb, kb, vb, rb, gb, 512)
    y_p = _merge(o_lat, ga, yb, xp, w, tm, True)

    qc_s, kc_s, ckv_s, kpe_s, ga_s, gb_s, qb_s, kb_s, vb_s, kbt_s, vbt_s = _proj(xs, *tabs_s, w, t)
    o_lat_s = _mla_sample(qc_s, c_ckv, c_kpe, kc_s)
    yb_s = _band_sample(qb_s, kb_s, vb_s, c_kb.reshape(bs, kl, B_WIDTH), c_vb.reshape(bs, kl, B_WIDTH),
                        rb, gb_s)
    y_s = _merge(o_lat_s, ga_s, yb_s, xs, w, t, False)

    kt = kbt.shape[1]
    outs_p = (ckv, kpe, kbt.reshape(b, kt, B_HEADS, HEAD_DIM), vbt.reshape(b, kt, B_HEADS, HEAD_DIM))
    outs_s = (ckv_s, kpe_s, kbt_s.reshape(bs, t, B_HEADS, HEAD_DIM), vbt_s.reshape(bs, t, B_HEADS, HEAD_DIM))
    return y_p, y_s, outs_p, outs_s


def kernel(x_prompt, x_sample, cache_ckv, cache_kpe, cache_kb, cache_vb, w_in, g_mix, g_cq, w_uq,
           g_ckv, w_uk, w_uv, rel_bias, w_out, g_final):
    depth = w_in.shape[0]
    assert depth == 1, "final norm is fused into the single layer's merge kernel"
    s = x_prompt.shape[1]
    t = x_sample.shape[1]
    past = cache_ckv.shape[2]
    tabs_p = _rope_tables(jnp.arange(s))
    tabs_s = _rope_tables(past + jnp.arange(t))
    w = _prep_weights(w_in[0], g_mix[0], g_cq[0], w_uq[0], g_ckv[0], w_uk[0], w_uv[0], w_out[0], g_final)
    y_p, y_s, outs_p, outs_s = _layer(x_prompt, x_sample, cache_ckv[0], cache_kpe[0], cache_kb[0],
                                      cache_vb[0], w, rel_bias[0], tabs_p, tabs_s)
    return (y_p, y_s) + tuple(o[None] for o in outs_p) + tuple(o[None] for o in outs_s)
```

```python
import functools

import jax
import jax.numpy as jnp
import numpy as np
from jax import lax
from jax.experimental import pallas as pl
from jax.experimental.pallas import tpu as pltpu

D_MODEL = 1024
CHUNK = 64
HEAD_DIM = 64
A_HEADS = 8
Q_LORA = 256
KV_LORA = 128
NOPE_DIM = 64
ROPE_DIM = 32
ROPE_BASE = 10000.0
MLA_SCALE = (NOPE_DIM + ROPE_DIM) ** -0.5
LOG2E = 1.4426950408889634
Q_SCALE = MLA_SCALE * LOG2E
B_HEADS = 8
B_WIDTH = B_HEADS * HEAD_DIM
A_WIDTH = A_HEADS * HEAD_DIM
BAND_CHUNKS = 8
MAX_REL = 128
B_SCALE = HEAD_DIM ** -0.5
EPS = 1e-6
NEG = -1e30

OFF_CQ = 0
OFF_CKV = OFF_CQ + Q_LORA
OFF_KR = OFF_CKV + KV_LORA
OFF_GA = OFF_KR + ROPE_DIM
OFF_QB = OFF_GA + A_WIDTH
OFF_KB = OFF_QB + B_WIDTH
OFF_VB = OFF_KB + B_WIDTH
OFF_GB = OFF_VB + B_WIDTH

LANE = 128
N_PAIR = B_HEADS // 2
QK_WIDTH = 2 * LANE
BAND_KEYS = BAND_CHUNKS * CHUNK
SUB_Q = 2 * CHUNK
SUB_W = BAND_KEYS + SUB_Q
ROLL_W = SUB_W + LANE
VMEM_LIMIT = 56 * 1024 * 1024

BF16 = jnp.bfloat16
F32 = jnp.float32


def _dot(a, b):
    return jnp.dot(a, b, preferred_element_type=F32)


def _dot_t(a, b):
    return lax.dot_general(a, b, (((1,), (1,)), ((), ())), preferred_element_type=F32)


def _silu(g):
    return g / (1.0 + jnp.exp(-g))


def _rms(x, g):
    return x * lax.rsqrt(jnp.mean(x * x, axis=-1, keepdims=True) + EPS) * g


def _proj_kernel(x_ref, cos_ref, sin_ref, gmix_ref, wcq_ref, wkv_ref, wga_ref, wqb_ref,
                 wkb_ref, wvb_ref, wgb_ref, gcq_ref, wqn_ref, wqpe_ref, wqpr_ref,
                 gckv_ref, wuk_ref,
                 qc_ref, kc_ref, ckv_ref, kpe_ref, ga_ref, gb_ref, qb_ref, kb_ref, vb_ref,
                 kbt_ref, vbt_ref, *maybe_vt_ref, tail_blocks):
    i = pl.program_id(1)
    n_i = pl.num_programs(1)
    x = x_ref[0]
    xn = _rms(x, gmix_ref[...]).astype(BF16)
    cos = cos_ref[...]
    sin = sin_ref[...]

    cq = _rms(_dot(xn, wcq_ref[...]), gcq_ref[...]).astype(BF16)
    qn = _dot(cq, wqn_ref[...]).astype(BF16)
    qpe = _dot(cq, wqpe_ref[...])
    qpr = _dot(cq, wqpr_ref[...])
    for h in range(A_HEADS):
        p = h // 2
        q_lat = _dot(qn[:, p * LANE:(p + 1) * LANE], wuk_ref[h]) * Q_SCALE
        sl = slice(h * LANE, (h + 1) * LANE)
        q_pe = (qpe[:, sl] * cos + qpr[:, sl] * sin) * Q_SCALE
        qc_ref[0, h, :, 0:LANE] = q_lat.astype(BF16)
        qc_ref[0, h, :, LANE:QK_WIDTH] = q_pe.astype(BF16)

    zkv = _dot(xn, wkv_ref[...])
    ckv = _rms(zkv[:, 0:LANE], gckv_ref[...])
    kpe = zkv[:, LANE:2 * LANE] * cos + zkv[:, 2 * LANE:3 * LANE] * sin
    ckv_ref[0] = ckv
    kpe_ref[0] = kpe[:, 0:ROPE_DIM]
    kc_ref[0, :, 0:LANE] = ckv.astype(BF16)
    kc_ref[0, :, LANE:QK_WIDTH] = kpe.astype(BF16)
    for vt_ref in maybe_vt_ref:
        vt_ref[0] = ckv.T.astype(BF16)

    ga_ref[0] = _dot(xn, wga_ref[...])
    gb_ref[0] = _dot(xn, wgb_ref[...])

    zq = _dot(xn, wqb_ref[...]) * B_SCALE
    zk = _dot(xn, wkb_ref[...])
    zv = _dot(xn, wvb_ref[...])
    for p in range(N_PAIR):
        sl = slice(p * LANE, (p + 1) * LANE)
        qb_ref[0, p] = zq[:, sl].astype(BF16)
        kb_ref[0, p] = zk[:, sl].astype(BF16)
        vb_ref[0, p] = zv[:, sl].astype(BF16)

    @pl.when(i >= n_i - tail_blocks)
    def _():
        kbt_ref[0] = zk
        vbt_ref[0] = zv


def _proj(x, cos, sin, w, tm):
    b, s, _ = x.shape
    n_i = s // tm
    kt = min(BAND_KEYS, s)
    tail_blocks = kt // tm
    assert s % tm == 0 and kt % tm == 0

    def full(a):
        nd = a.ndim
        return pl.BlockSpec(a.shape, lambda bi, i, _nd=nd: (0,) * _nd)

    weights = [w["g_mix"], w["w_cq"], w["w_kv"], w["w_ga"], w["w_qb"], w["w_kb"], w["w_vb"],
               w["w_gb"], w["g_cq"], w["w_qn"], w["w_qpe"], w["w_qpr"], w["g_ckv"], w["w_ukx"]]
    in_specs = [pl.BlockSpec((1, tm, D_MODEL), lambda bi, i: (bi, i, 0)),
                pl.BlockSpec((tm, LANE), lambda bi, i: (i, 0)),
                pl.BlockSpec((tm, LANE), lambda bi, i: (i, 0))] + [full(a) for a in weights]

    def tail_map(bi, i):
        return (bi, jnp.maximum(i - (n_i - tail_blocks), 0), 0)

    out_shape = [
        jax.ShapeDtypeStruct((b, A_HEADS, s, QK_WIDTH), BF16),
        jax.ShapeDtypeStruct((b, s, QK_WIDTH), BF16),
        jax.ShapeDtypeStruct((b, s, KV_LORA), F32),
        jax.ShapeDtypeStruct((b, s, ROPE_DIM), F32),
        jax.ShapeDtypeStruct((b, s, A_WIDTH), F32),
        jax.ShapeDtypeStruct((b, s, B_WIDTH), F32),
        jax.ShapeDtypeStruct((b, N_PAIR, s, LANE), BF16),
        jax.ShapeDtypeStruct((b, N_PAIR, s, LANE), BF16),
        jax.ShapeDtypeStruct((b, N_PAIR, s, LANE), BF16),
        jax.ShapeDtypeStruct((b, kt, B_WIDTH), F32),
        jax.ShapeDtypeStruct((b, kt, B_WIDTH), F32),
    ]
    out_specs = [
        pl.BlockSpec((1, A_HEADS, tm, QK_WIDTH), lambda bi, i: (bi, 0, i, 0)),
        pl.BlockSpec((1, tm, QK_WIDTH), lambda bi, i: (bi, i, 0)),
        pl.BlockSpec((1, tm, KV_LORA), lambda bi, i: (bi, i, 0)),
        pl.BlockSpec((1, tm, ROPE_DIM), lambda bi, i: (bi, i, 0)),
        pl.BlockSpec((1, tm, A_WIDTH), lambda bi, i: (bi, i, 0)),
        pl.BlockSpec((1, tm, B_WIDTH), lambda bi, i: (bi, i, 0)),
        pl.BlockSpec((1, N_PAIR, tm, LANE), lambda bi, i: (bi, 0, i, 0)),
        pl.BlockSpec((1, N_PAIR, tm, LANE), lambda bi, i: (bi, 0, i, 0)),
        pl.BlockSpec((1, N_PAIR, tm, LANE), lambda bi, i: (bi, 0, i, 0)),
        pl.BlockSpec((1, tm, B_WIDTH), tail_map),
        pl.BlockSpec((1, tm, B_WIDTH), tail_map),
    ]
    if tm % LANE == 0:
        out_shape.append(jax.ShapeDtypeStruct((b, KV_LORA, s), BF16))
        out_specs.append(pl.BlockSpec((1, KV_LORA, tm), lambda bi, i: (bi, 0, i)))
    return pl.pallas_call(
        functools.partial(_proj_kernel, tail_blocks=tail_blocks),
        grid=(b, n_i),
        in_specs=in_specs,
        out_specs=out_specs,
        out_shape=out_shape,
        compiler_params=pltpu.CompilerParams(
            dimension_semantics=("arbitrary", "arbitrary"), vmem_limit_bytes=VMEM_LIMIT),
        name="proj",
    )(x, cos, sin, *weights)


def _mla_kernel(q_ref, k_ref, vt_ref, o_ref, m_ref, l_ref, acc_ref, *, tq):
    qi = pl.program_id(1)
    m_ref[...] = jnp.full(m_ref.shape, NEG, F32)
    l_ref[...] = jnp.zeros(l_ref.shape, F32)
    acc_ref[...] = jnp.zeros(acc_ref.shape, F32)

    def step(ki, masked):
        k0 = pl.multiple_of(ki * tq, tq)
        k = k_ref[0, pl.ds(k0, tq), :]
        vt = vt_ref[0, :, pl.ds(k0, tq)]
        s_next = _dot_t(k, q_ref[0, 0])
        for h in range(A_HEADS):
            s = s_next
            if h + 1 < A_HEADS:
                s_next = _dot_t(k, q_ref[0, h + 1])
            if masked:
                key = lax.broadcasted_iota(jnp.int32, s.shape, 0) // CHUNK
                qry = lax.broadcasted_iota(jnp.int32, s.shape, 1) // CHUNK
                s = jnp.where(key <= qry, s, NEG)
            m_prev = m_ref[h]
            m_next = jnp.maximum(m_prev, jnp.max(s, axis=0, keepdims=True))
            alpha = jnp.exp2(m_prev - m_next)
            p = jnp.exp2(s - m_next)
            l_ref[h] = alpha * l_ref[h] + jnp.sum(p, axis=0, keepdims=True)
            m_ref[h] = m_next
            acc_ref[h] = alpha * acc_ref[h] + _dot(vt, p.astype(BF16))

    def full_block(ki, carry):
        step(ki, False)
        return carry

    lax.fori_loop(0, qi, full_block, 0)
    step(qi, True)
    for h in range(A_HEADS):
        o_ref[0, h] = (acc_ref[h] / l_ref[h]).astype(o_ref.dtype)


def _mla_prompt(qc, kc, vt, tq):
    b, h, s, _ = qc.shape
    return pl.pallas_call(
        functools.partial(_mla_kernel, tq=tq),
        grid=(b, s // tq),
        in_specs=[pl.BlockSpec((1, h, tq, QK_WIDTH), lambda bi, qi: (bi, 0, qi, 0)),
                  pl.BlockSpec((1, s, QK_WIDTH), lambda bi, qi: (bi, 0, 0)),
                  pl.BlockSpec((1, KV_LORA, s), lambda bi, qi: (bi, 0, 0))],
        out_specs=pl.BlockSpec((1, h, KV_LORA, tq), lambda bi, qi: (bi, 0, 0, qi)),
        out_shape=jax.ShapeDtypeStruct((b, h, KV_LORA, s), BF16),
        scratch_shapes=[pltpu.VMEM((h, 1, tq), F32), pltpu.VMEM((h, 1, tq), F32),
                        pltpu.VMEM((h, KV_LORA, tq), F32)],
        compiler_params=pltpu.CompilerParams(
            dimension_semantics=("arbitrary", "arbitrary"), vmem_limit_bytes=VMEM_LIMIT),
        name="mla_prompt",
    )(qc, kc, vt)


def _mla_sample_kernel(q_ref, ckv_ref, kpe_ref, kn_ref, o_ref):
    hh, t, _ = q_ref.shape[1:]
    q = q_ref[0].reshape(hh * t, QK_WIDTH)
    ckv_c = ckv_ref[0].astype(BF16)
    kpe_c = kpe_ref[0].astype(BF16)
    kn = kn_ref[0]
    s_c = _dot_t(q[:, 0:KV_LORA], ckv_c) + _dot_t(q[:, KV_LORA:KV_LORA + ROPE_DIM], kpe_c)
    s_n = _dot_t(q, kn)
    m = jnp.maximum(jnp.max(s_c, axis=-1, keepdims=True), jnp.max(s_n, axis=-1, keepdims=True))
    p_c = jnp.exp2(s_c - m)
    p_n = jnp.exp2(s_n - m)
    l = jnp.sum(p_c, axis=-1, keepdims=True) + jnp.sum(p_n, axis=-1, keepdims=True)
    o = (_dot(p_c.astype(BF16), ckv_c) + _dot(p_n.astype(BF16), kn[:, 0:KV_LORA])) / l
    o_ref[0] = o.reshape(hh, t, KV_LORA).astype(o_ref.dtype)


def _mla_sample(qc, cache_ckv, cache_kpe, kc):
    b, h, t, _ = qc.shape
    past = cache_ckv.shape[1]
    return pl.pallas_call(
        _mla_sample_kernel,
        grid=(b,),
        in_specs=[pl.BlockSpec((1, h, t, QK_WIDTH), lambda bi: (bi, 0, 0, 0)),
                  pl.BlockSpec((1, past, KV_LORA), lambda bi: (bi, 0, 0)),
                  pl.BlockSpec((1, past, ROPE_DIM), lambda bi: (bi, 0, 0)),
                  pl.BlockSpec((1, t, QK_WIDTH), lambda bi: (bi, 0, 0))],
        out_specs=pl.BlockSpec((1, h, t, KV_LORA), lambda bi: (bi, 0, 0, 0)),
        out_shape=jax.ShapeDtypeStruct((b, h, t, KV_LORA), BF16),
        compiler_params=pltpu.CompilerParams(
            dimension_semantics=("arbitrary",), vmem_limit_bytes=VMEM_LIMIT),
        name="mla_sample",
    )(qc, cache_ckv, cache_kpe, kc)


def _pair_rows(q):
    lane = lax.broadcasted_iota(jnp.int32, q.shape, 1)
    zero = jnp.zeros_like(q)
    return jnp.concatenate([jnp.where(lane < HEAD_DIM, q, zero),
                            jnp.where(lane >= HEAD_DIM, q, zero)], axis=0)


def _pair_merge(o2, r):
    lane = lax.broadcasted_iota(jnp.int32, (r, LANE), 1)
    return jnp.where(lane < HEAD_DIM, o2[0:r], o2[r:2 * r])


def _rel_bias_rows(rb_ref, h, rows):
    base = jnp.broadcast_to(rb_ref[h:h + 1, :], (rows, ROLL_W))
    return pltpu.roll(base, 0, 1, stride=1, stride_axis=0)


def _band_kernel(q_ref, kp_ref, kcur_ref, vp_ref, vcur_ref, rb_ref, g_ref, y_ref,
                 k_scr, v_scr, bias_ref, *, tq):
    i = pl.program_id(1)

    @pl.when(jnp.logical_and(pl.program_id(0) == 0, i == 0))
    def _():
        a = lax.broadcasted_iota(jnp.int32, (SUB_Q, SUB_W), 0) // CHUNK
        j = lax.broadcasted_iota(jnp.int32, (SUB_Q, SUB_W), 1) // CHUNK
        back = a + BAND_CHUNKS - j
        visible = jnp.logical_and(back >= 0, back <= BAND_CHUNKS)
        for h in range(B_HEADS):
            tile = _rel_bias_rows(rb_ref, h, SUB_Q)[:, 0:SUB_W]
            bias_ref[h // 2, (h % 2) * SUB_Q:(h % 2 + 1) * SUB_Q, :] = jnp.where(visible, tile, NEG)

    k_scr[:, 0:tq, :] = kp_ref[0]
    k_scr[:, tq:2 * tq, :] = kcur_ref[0]
    v_scr[:, 0:tq, :] = vp_ref[0]
    v_scr[:, tq:2 * tq, :] = vcur_ref[0]
    has_prev = i > 0
    col = lax.broadcasted_iota(jnp.int32, (2 * SUB_Q, SUB_W), 1)
    for r in range(tq // SUB_Q):
        lo = tq - BAND_KEYS + r * SUB_Q
        in_cur = col >= (tq - lo)
        ok = jnp.logical_or(in_cur, has_prev)
        rows = slice(r * SUB_Q, (r + 1) * SUB_Q)
        for p in range(N_PAIR):
            q2 = _pair_rows(q_ref[0, p, rows, :])
            kw = k_scr[p, lo:lo + SUB_W, :]
            vw = v_scr[p, lo:lo + SUB_W, :]
            s = _dot_t(q2, kw) + bias_ref[p]
            s = jnp.where(ok, s, NEG)
            m = jnp.max(s, axis=-1, keepdims=True)
            e = jnp.exp(s - m)
            l = jnp.sum(e, axis=-1, keepdims=True)
            o2 = _dot(e.astype(BF16), vw) / l
            o = _pair_merge(o2, SUB_Q)
            g = g_ref[0, rows, p * LANE:(p + 1) * LANE]
            y_ref[0, rows, p * LANE:(p + 1) * LANE] = (o * _silu(g)).astype(y_ref.dtype)


def _band_prompt(qb, kb, vb, rb, gb, tq):
    b, np_, s, _ = qb.shape
    n = s // tq
    assert tq >= BAND_KEYS and tq % SUB_Q == 0

    def cur(bi, i):
        return (bi, 0, i, 0)

    def prev(bi, i):
        return (bi, 0, jnp.maximum(i - 1, 0), 0)

    blk = (1, np_, tq, LANE)
    return pl.pallas_call(
        functools.partial(_band_kernel, tq=tq),
        grid=(b, n),
        in_specs=[pl.BlockSpec(blk, cur), pl.BlockSpec(blk, prev), pl.BlockSpec(blk, cur),
                  pl.BlockSpec(blk, prev), pl.BlockSpec(blk, cur),
                  pl.BlockSpec(rb.shape, lambda bi, i: (0, 0)),
                  pl.BlockSpec((1, tq, B_WIDTH), lambda bi, i: (bi, i, 0))],
        out_specs=pl.BlockSpec((1, tq, B_WIDTH), lambda bi, i: (bi, i, 0)),
        out_shape=jax.ShapeDtypeStruct((b, s, B_WIDTH), BF16),
        scratch_shapes=[pltpu.VMEM((np_, 2 * tq, LANE), BF16), pltpu.VMEM((np_, 2 * tq, LANE), BF16),
                        pltpu.VMEM((np_, 2 * SUB_Q, SUB_W), F32)],
        compiler_params=pltpu.CompilerParams(
            dimension_semantics=("arbitrary", "arbitrary"), vmem_limit_bytes=VMEM_LIMIT),
        name="band_prompt",
    )(qb, kb, kb, vb, vb, rb, gb)


def _band_sample_kernel(q_ref, kn_ref, vn_ref, kc_ref, vc_ref, rb_ref, g_ref, y_ref, bias_ref):
    t = q_ref.shape[2]
    kl = kc_ref.shape[1]

    @pl.when(pl.program_id(0) == 0)
    def _():
        for h in range(B_HEADS):
            bias_ref[h // 2, (h % 2) * t:(h % 2 + 1) * t, :] = _rel_bias_rows(rb_ref, h, t)

    for p in range(N_PAIR):
        sl = slice(p * LANE, (p + 1) * LANE)
        q2 = _pair_rows(q_ref[0, p])
        kc = kc_ref[0, :, sl].astype(BF16)
        vc = vc_ref[0, :, sl].astype(BF16)
        kn = kn_ref[0, p]
        vn = vn_ref[0, p]
        s_c = _dot_t(q2, kc) + bias_ref[p, :, 0:kl]
        s_n = _dot_t(q2, kn) + bias_ref[p, :, kl:kl + t]
        m = jnp.maximum(jnp.max(s_c, axis=-1, keepdims=True), jnp.max(s_n, axis=-1, keepdims=True))
        e_c = jnp.exp(s_c - m)
        e_n = jnp.exp(s_n - m)
        l = jnp.sum(e_c, axis=-1, keepdims=True) + jnp.sum(e_n, axis=-1, keepdims=True)
        o2 = (_dot(e_c.astype(BF16), vc) + _dot(e_n.astype(BF16), vn)) / l
        o = _pair_merge(o2, t)
        y_ref[0, :, sl] = (o * _silu(g_ref[0, :, sl])).astype(y_ref.dtype)


def _band_sample(qb, kb, vb, cache_kb, cache_vb, rb, gb):
    b, np_, t, _ = qb.shape
    kl = cache_kb.shape[1]
    assert kl == BAND_KEYS and kl + t <= ROLL_W - LANE
    new = pl.BlockSpec((1, np_, t, LANE), lambda bi: (bi, 0, 0, 0))
    cache = pl.BlockSpec((1, kl, B_WIDTH), lambda bi: (bi, 0, 0))
    return pl.pallas_call(
        _band_sample_kernel,
        grid=(b,),
        in_specs=[new, new, new, cache, cache,
                  pl.BlockSpec(rb.shape, lambda bi: (0, 0)),
                  pl.BlockSpec((1, t, B_WIDTH), lambda bi: (bi, 0, 0))],
        out_specs=pl.BlockSpec((1, t, B_WIDTH), lambda bi: (bi, 0, 0)),
        out_shape=jax.ShapeDtypeStruct((b, t, B_WIDTH), BF16),
        scratch_shapes=[pltpu.VMEM((np_, 2 * t, ROLL_W), F32)],
        compiler_params=pltpu.CompilerParams(
            dimension_semantics=("arbitrary",), vmem_limit_bytes=VMEM_LIMIT),
        name="band_sample",
    )(qb, kb, vb, cache_kb, cache_vb, rb, gb)


def _merge_kernel(ol_ref, ga_ref, yb_ref, x_ref, wuv_ref, woa_ref, wob_ref, gf_ref, y_ref, *,
                  latent_major):
    def up(h):
        if latent_major:
            return lax.dot_general(ol_ref[0, h], wuv_ref[h], (((0,), (0,)), ((), ())),
                                   preferred_element_type=F32)
        return _dot(ol_ref[0, h], wuv_ref[h])

    parts = []
    for p in range(A_HEADS // 2):
        parts.append(up(2 * p) + up(2 * p + 1))
    o_a = jnp.concatenate(parts, axis=-1)
    ya = (o_a * _silu(ga_ref[0])).astype(BF16)
    acc = _dot(ya, woa_ref[...]) + _dot(yb_ref[0], wob_ref[...]) + x_ref[0]
    y_ref[0] = _rms(acc, gf_ref[...])


def _merge(o_lat, ga, yb, x, w, tm, latent_major):
    b, s, _ = x.shape
    n_i = s // tm
    if latent_major:
        ol_spec = pl.BlockSpec((1, A_HEADS, KV_LORA, tm), lambda bi, i: (bi, 0, 0, i))
    else:
        ol_spec = pl.BlockSpec((1, A_HEADS, tm, KV_LORA), lambda bi, i: (bi, 0, i, 0))

    def full(a):
        nd = a.ndim
        return pl.BlockSpec(a.shape, lambda bi, i, _nd=nd: (0,) * _nd)

    weights = [w["w_uvx"], w["w_out_a"], w["w_out_b"], w["g_final"]]
    return pl.pallas_call(
        functools.partial(_merge_kernel, latent_major=latent_major),
        grid=(b, n_i),
        in_specs=[ol_spec,
                  pl.BlockSpec((1, tm, A_WIDTH), lambda bi, i: (bi, i, 0)),
                  pl.BlockSpec((1, tm, B_WIDTH), lambda bi, i: (bi, i, 0)),
                  pl.BlockSpec((1, tm, D_MODEL), lambda bi, i: (bi, i, 0))] + [full(a) for a in weights],
        out_specs=pl.BlockSpec((1, tm, D_MODEL), lambda bi, i: (bi, i, 0)),
        out_shape=jax.ShapeDtypeStruct((b, s, D_MODEL), F32),
        compiler_params=pltpu.CompilerParams(
            dimension_semantics=("arbitrary", "arbitrary"), vmem_limit_bytes=VMEM_LIMIT),
        name="merge",
    )(o_lat, ga, yb, x, *weights)


def _rope_tables(pos):
    half = ROPE_DIM // 2
    inv = ROPE_BASE ** (-jnp.arange(half, dtype=F32) / half)
    ang = pos.astype(F32)[:, None] * inv
    cos = jnp.concatenate([jnp.cos(ang), jnp.cos(ang)], axis=-1)
    sin = jnp.concatenate([jnp.sin(ang), jnp.sin(ang)], axis=-1)
    pad = ((0, 0), (0, LANE - ROPE_DIM))
    return jnp.pad(cos, pad), jnp.pad(sin, pad)


def _rot_cols(w):
    half = ROPE_DIM // 2
    return jnp.concatenate([-w[..., half:], w[..., :half]], axis=-1)


def _pad_lanes(w, width):
    return jnp.pad(w, [(0, 0)] * (w.ndim - 1) + [(0, width - w.shape[-1])])


def _prep_weights(w_in, g_mix, g_cq, w_uq, g_ckv, w_uk, w_uv, w_out, g_final):
    w_kr = w_in[:, OFF_KR:OFF_GA]
    w_kv = jnp.concatenate([w_in[:, OFF_CKV:OFF_KR], _pad_lanes(w_kr, LANE),
                            _pad_lanes(_rot_cols(w_kr), LANE)], axis=-1)
    w_pe = w_uq[:, :, NOPE_DIM:]
    zeros64 = jnp.zeros((A_HEADS, HEAD_DIM, KV_LORA), F32)
    uk_t = jnp.transpose(w_uk, (1, 2, 0))
    even = (jnp.arange(A_HEADS) % 2 == 0)[:, None, None]
    w_ukx = jnp.where(even, jnp.concatenate([uk_t, zeros64], axis=1),
                      jnp.concatenate([zeros64, uk_t], axis=1))
    uv = jnp.transpose(w_uv, (1, 0, 2))
    zeros_uv = jnp.zeros_like(uv)
    w_uvx = jnp.where(even, jnp.concatenate([uv, zeros_uv], axis=2),
                      jnp.concatenate([zeros_uv, uv], axis=2))
    return {
        "g_mix": g_mix.reshape(1, D_MODEL),
        "w_cq": w_in[:, OFF_CQ:OFF_CKV].astype(BF16),
        "w_kv": w_kv.astype(BF16),
        "w_ga": w_in[:, OFF_GA:OFF_QB].astype(BF16),
        "w_qb": w_in[:, OFF_QB:OFF_KB].astype(BF16),
        "w_kb": w_in[:, OFF_KB:OFF_VB].astype(BF16),
        "w_vb": w_in[:, OFF_VB:OFF_GB].astype(BF16),
        "w_gb": w_in[:, OFF_GB:].astype(BF16),
        "g_cq": g_cq.reshape(1, Q_LORA),
        "w_qn": w_uq[:, :, :NOPE_DIM].reshape(Q_LORA, A_WIDTH).astype(BF16),
        "w_qpe": _pad_lanes(w_pe, LANE).reshape(Q_LORA, A_HEADS * LANE).astype(BF16),
        "w_qpr": _pad_lanes(_rot_cols(w_pe), LANE).reshape(Q_LORA, A_HEADS * LANE).astype(BF16),
        "g_ckv": g_ckv.reshape(1, KV_LORA),
        "w_ukx": w_ukx.astype(BF16),
        "w_uvx": w_uvx.astype(BF16),
        "w_out_a": w_out[:A_WIDTH].astype(BF16),
        "w_out_b": w_out[A_WIDTH:].astype(BF16),
        "g_final": g_final.reshape(1, D_MODEL),
    }


def _rel_bias_vector(rel_bias):
    far = rel_bias[:, 2 * MAX_REL:]
    n_far = BAND_KEYS - MAX_REL + 1
    near = rel_bias[:, 2 * MAX_REL - 1:0:-1]
    n_tail = ROLL_W - n_far - near.shape[1]
    h = rel_bias.shape[0]
    return jnp.concatenate([jnp.broadcast_to(far, (h, n_far)), near,
                            jnp.broadcast_to(far, (h, n_tail))], axis=1).astype(F32)


def _layer(xp, xs, c_ckv, c_kpe, c_kb, c_vb, w, rel_bias, tabs_p, tabs_s):
    b, s, _ = xp.shape
    bs, t, _ = xs.shape
    kl = c_kb.shape[1]
    tm = 512

    qc, kc, ckv, kpe, ga, gb, qb, kb, vb, kbt, vbt, vt = _proj(xp, *tabs_p, w, tm)
    o_lat = _mla_prompt(qc, kc, vt, 512)
    rb = _rel_bias_vector(rel_bias)
    yb = _band_prompt(qb, kb, vb, rb, gb, 512)
    y_p = _merge(o_lat, ga, yb, xp, w, tm, True)

    qc_s, kc_s, ckv_s, kpe_s, ga_s, gb_s, qb_s, kb_s, vb_s, kbt_s, vbt_s = _proj(xs, *tabs_s, w, t)
    o_lat_s = _mla_sample(qc_s, c_ckv, c_kpe, kc_s)
    yb_s = _band_sample(qb_s, kb_s, vb_s, c_kb.reshape(bs, kl, B_WIDTH), c_vb.reshape(bs, kl, B_WIDTH),
                        rb, gb_s)
    y_s = _merge(o_lat_s, ga_s, yb_s, xs, w, t, False)

    kt = kbt.shape[1]
    outs_p = (ckv, kpe, kbt.reshape(b, kt, B_HEADS, HEAD_DIM), vbt.reshape(b, kt, B_HEADS, HEAD_DIM))
    outs_s = (ckv_s, kpe_s, kbt_s.reshape(bs, t, B_HEADS, HEAD_DIM), vbt_s.reshape(bs, t, B_HEADS, HEAD_DIM))
    return y_p, y_s, outs_p, outs_s


def kernel(x_prompt, x_sample, cache_ckv, cache_kpe, cache_kb, cache_vb, w_in, g_mix, g_cq, w_uq,
           g_ckv, w_uk, w_uv, rel_bias, w_out, g_final):
    depth = w_in.shape[0]
    assert depth == 1, "final norm is fused into the single layer's merge kernel"
    s = x_prompt.shape[1]
    t = x_sample.shape[1]
    past = cache_ckv.shape[2]
    tabs_p = _rope_tables(jnp.arange(s))
    tabs_s = _rope_tables(past + jnp.arange(t))
    w = _prep_weights(w_in[0], g_mix[0], g_cq[0], w_uq[0], g_ckv[0], w_uk[0], w_uv[0], w_out[0], g_final)
    y_p, y_s, outs_p, outs_s = _layer(x_prompt, x_sample, cache_ckv[0], cache_kpe[0], cache_kb[0],
                                      cache_vb[0], w, rel_bias[0], tabs_p, tabs_s)
    return (y_p, y_s) + tuple(o[None] for o in outs_p) + tuple(o[None] for o in outs_s)
```

```python
import functools

import jax
import jax.numpy as jnp
from jax import lax
from jax.experimental import pallas as pl
from jax.experimental.pallas import tpu as pltpu

D_MODEL = 1024
CHUNK = 64
HEAD_DIM = 64
A_HEADS = 8
Q_LORA = 256
KV_LORA = 128
NOPE_DIM = 64
ROPE_DIM = 32
ROPE_BASE = 10000.0
MLA_SCALE = (NOPE_DIM + ROPE_DIM) ** -0.5
B_HEADS = 8
B_WIDTH = B_HEADS * HEAD_DIM
A_WIDTH = A_HEADS * HEAD_DIM
BAND_CHUNKS = 8
MAX_REL = 128
B_SCALE = HEAD_DIM ** -0.5
EPS = 1e-6
NEG = -1e30
LOG2E = 1.4426950408889634

OFF_CQ = 0
OFF_CKV = OFF_CQ + Q_LORA
OFF_KR = OFF_CKV + KV_LORA
OFF_GA = OFF_KR + ROPE_DIM
OFF_QB = OFF_GA + A_WIDTH
OFF_KB = OFF_QB + B_WIDTH
OFF_VB = OFF_KB + B_WIDTH
OFF_GB = OFF_VB + B_WIDTH

LANE = 128
N_PAIR = B_HEADS // 2
QK_WIDTH = 2 * LANE
ROPE_PER_GROUP = LANE // ROPE_DIM
BAND_KEYS = BAND_CHUNKS * CHUNK
SUB_Q = 2 * CHUNK
SUB_W = BAND_KEYS + SUB_Q
ROLL_W = SUB_W + LANE
VMEM_LIMIT = 56 * 1024 * 1024
BAND_GROUP = 2
PROMPT_ROWS = 512

BF16 = jnp.bfloat16
F32 = jnp.float32


def _dot(a, b):
    return jnp.dot(a, b, preferred_element_type=F32)


def _dot_t(a, b):
    return lax.dot_general(a, b, (((1,), (1,)), ((), ())), preferred_element_type=F32)


def _dot_lhs_t(a, b):
    return lax.dot_general(a, b, (((0,), (0,)), ((), ())), preferred_element_type=F32)


def _silu(g):
    return g / (1.0 + jnp.exp(-g))


def _rms(x, g):
    return x * lax.rsqrt(jnp.mean(x * x, axis=-1, keepdims=True) + EPS) * g


def _full_spec(a, n_grid):
    nd = a.ndim
    if n_grid == 1:
        return pl.BlockSpec(a.shape, lambda bi, _nd=nd: (0,) * _nd)
    return pl.BlockSpec(a.shape, lambda bi, i, _nd=nd: (0,) * _nd)


def _params(n_grid):
    return pltpu.CompilerParams(dimension_semantics=("arbitrary",) * n_grid,
                                vmem_limit_bytes=VMEM_LIMIT)


def _proj_kernel(x_ref, cos_ref, sin_ref, gmix_ref, wcq_ref, wkv_ref, wga_ref, wqb_ref,
                 wkb_ref, wvb_ref, wgb_ref, gcq_ref, wqn_ref, wqpp_ref, gckv_ref, wuk_ref,
                 qc_ref, kc_ref, ckv_ref, kpe_ref, qb_ref, kb_ref, kbt_ref, vbt_ref,
                 ga_ref, gb_ref, vb_ref, *maybe_vt_ref, tail_blocks, feature_major):
    i = pl.program_id(1)
    n_i = pl.num_programs(1)
    x = x_ref[0]
    xn = _rms(x, gmix_ref[...]).astype(BF16)
    cos = cos_ref[...]
    sin = sin_ref[...]

    cq = _rms(_dot(xn, wcq_ref[...]), gcq_ref[...]).astype(BF16)
    qn = _dot(cq, wqn_ref[...]).astype(BF16)
    qpp = _dot(cq, wqpp_ref[...])
    half = A_HEADS * ROPE_DIM
    for p in range(A_HEADS // 2):
        q_lat = _dot(qn[:, p * LANE:(p + 1) * LANE], wuk_ref[p]) * (MLA_SCALE * LOG2E)
        qc_ref[0, 2 * p, :, 0:LANE] = q_lat[:, 0:LANE].astype(BF16)
        qc_ref[0, 2 * p + 1, :, 0:LANE] = q_lat[:, LANE:2 * LANE].astype(BF16)
    for c in range(half // LANE):
        sl = slice(c * LANE, (c + 1) * LANE)
        sr = slice(half + c * LANE, half + (c + 1) * LANE)
        dense = (qpp[:, sl] * cos + qpp[:, sr] * sin) * (MLA_SCALE * LOG2E)
        for k in range(ROPE_PER_GROUP):
            q_pe = dense if k == 0 else pltpu.roll(dense, LANE - ROPE_DIM * k, 1)
            qc_ref[0, c * ROPE_PER_GROUP + k, :, LANE:QK_WIDTH] = q_pe.astype(BF16)

    zkv = _dot(xn, wkv_ref[...])
    ckv = _rms(zkv[:, 0:LANE], gckv_ref[...])
    kpe = zkv[:, LANE:2 * LANE] * cos + zkv[:, 2 * LANE:3 * LANE] * sin
    ckv_ref[0] = ckv
    kpe_ref[0] = kpe[:, 0:ROPE_DIM]
    kc_ref[0, :, 0:LANE] = ckv.astype(BF16)
    kc_ref[0, :, LANE:QK_WIDTH] = kpe.astype(BF16)
    for vt_ref in maybe_vt_ref:
        vt_ref[0] = ckv.T.astype(BF16)

    g_a = _dot(xn, wga_ref[...])
    g_b = _dot(xn, wgb_ref[...])
    zq = _dot(xn, wqb_ref[...]) * (B_SCALE * LOG2E)
    zk = _dot(xn, wkb_ref[...])
    zv = _dot(xn, wvb_ref[...])
    if feature_major:
        ga_ref[0] = g_a.T
        gb_ref[0] = g_b.T
    else:
        ga_ref[0] = g_a
        gb_ref[0] = g_b
    for p in range(N_PAIR):
        sl = slice(p * LANE, (p + 1) * LANE)
        qb_ref[0, p] = zq[:, sl].astype(BF16)
        kb_ref[0, p] = zk[:, sl].astype(BF16)
        if feature_major:
            vb_ref[0, p] = zv[:, sl].T.astype(BF16)
        else:
            vb_ref[0, p] = zv[:, sl].astype(BF16)

    @pl.when(i >= n_i - tail_blocks)
    def _():
        kbt_ref[0] = zk
        vbt_ref[0] = zv


def _proj(x, cos, sin, w, tm, feature_major):
    b, s, _ = x.shape
    n_i = s // tm
    kt = min(BAND_KEYS, s)
    tail_blocks = kt // tm
    assert s % tm == 0 and kt % tm == 0

    weights = [w["g_mix"], w["w_cq"], w["w_kv"], w["w_ga"], w["w_qb"], w["w_kb"], w["w_vb"],
               w["w_gb"], w["g_cq"], w["w_qn"], w["w_qpp"], w["g_ckv"], w["w_ukp"]]
    in_specs = [pl.BlockSpec((1, tm, D_MODEL), lambda bi, i: (bi, i, 0)),
                pl.BlockSpec((tm, LANE), lambda bi, i: (i, 0)),
                pl.BlockSpec((tm, LANE), lambda bi, i: (i, 0))] + [_full_spec(a, 2) for a in weights]

    def rows3(width):
        return pl.BlockSpec((1, tm, width), lambda bi, i: (bi, i, 0))

    def tail_map(bi, i):
        return (bi, jnp.maximum(i - (n_i - tail_blocks), 0), 0)

    pair_rows = pl.BlockSpec((1, N_PAIR, tm, LANE), lambda bi, i: (bi, 0, i, 0))
    out_shape = [
        jax.ShapeDtypeStruct((b, A_HEADS, s, QK_WIDTH), BF16),
        jax.ShapeDtypeStruct((b, s, QK_WIDTH), BF16),
        jax.ShapeDtypeStruct((b, s, KV_LORA), F32),
        jax.ShapeDtypeStruct((b, s, ROPE_DIM), F32),
        jax.ShapeDtypeStruct((b, N_PAIR, s, LANE), BF16),
        jax.ShapeDtypeStruct((b, N_PAIR, s, LANE), BF16),
        jax.ShapeDtypeStruct((b, kt, B_WIDTH), F32),
        jax.ShapeDtypeStruct((b, kt, B_WIDTH), F32),
    ]
    out_specs = [
        pl.BlockSpec((1, A_HEADS, tm, QK_WIDTH), lambda bi, i: (bi, 0, i, 0)),
        rows3(QK_WIDTH), rows3(KV_LORA), rows3(ROPE_DIM), pair_rows, pair_rows,
        pl.BlockSpec((1, tm, B_WIDTH), tail_map),
        pl.BlockSpec((1, tm, B_WIDTH), tail_map),
    ]
    if feature_major:
        cols = pl.BlockSpec((1, A_WIDTH, tm), lambda bi, i: (bi, 0, i))
        out_shape += [jax.ShapeDtypeStruct((b, A_WIDTH, s), F32),
                      jax.ShapeDtypeStruct((b, B_WIDTH, s), F32),
                      jax.ShapeDtypeStruct((b, N_PAIR, LANE, s), BF16),
                      jax.ShapeDtypeStruct((b, KV_LORA, s), BF16)]
        out_specs += [cols, cols,
                      pl.BlockSpec((1, N_PAIR, LANE, tm), lambda bi, i: (bi, 0, 0, i)),
                      pl.BlockSpec((1, KV_LORA, tm), lambda bi, i: (bi, 0, i))]
    else:
        out_shape += [jax.ShapeDtypeStruct((b, s, A_WIDTH), F32),
                      jax.ShapeDtypeStruct((b, s, B_WIDTH), F32),
                      jax.ShapeDtypeStruct((b, N_PAIR, s, LANE), BF16)]
        out_specs += [rows3(A_WIDTH), rows3(B_WIDTH), pair_rows]
    return pl.pallas_call(
        functools.partial(_proj_kernel, tail_blocks=tail_blocks, feature_major=feature_major),
        grid=(b, n_i),
        in_specs=in_specs,
        out_specs=out_specs,
        out_shape=out_shape,
        compiler_params=_params(2),
        name="proj",
    )(x, cos, sin, *weights)


def _mla_kernel(q_ref, k_ref, vt_ref, gt_ref, wuv_ref, y_ref, m_ref, l_ref, acc_ref, *, tq):
    qi = pl.program_id(1)
    m_ref[...] = jnp.full(m_ref.shape, NEG, F32)
    l_ref[...] = jnp.zeros(l_ref.shape, F32)
    acc_ref[...] = jnp.zeros(acc_ref.shape, F32)

    def step(ki, masked):
        k0 = pl.multiple_of(ki * tq, tq)
        k = k_ref[0, pl.ds(k0, tq), :]
        vt = vt_ref[0, :, pl.ds(k0, tq)]
        s_next = _dot_t(k, q_ref[0, 0])
        for h in range(A_HEADS):
            s = s_next
            if h + 1 < A_HEADS:
                s_next = _dot_t(k, q_ref[0, h + 1])
            if masked:
                key = lax.broadcasted_iota(jnp.int32, s.shape, 0) // CHUNK
                qry = lax.broadcasted_iota(jnp.int32, s.shape, 1) // CHUNK
                s = jnp.where(key <= qry, s, NEG)
            m_prev = m_ref[h]
            m_next = jnp.maximum(m_prev, jnp.max(s, axis=0, keepdims=True))
            alpha = jnp.exp2(m_prev - m_next)
            p = jnp.exp2(s - m_next)
            l_ref[h] = alpha * l_ref[h] + jnp.sum(p, axis=0, keepdims=True)
            m_ref[h] = m_next
            acc_ref[h] = alpha * acc_ref[h] + _dot(vt, p.astype(BF16))

    def full_block(ki, carry):
        step(ki, False)
        return carry

    lax.fori_loop(0, qi, full_block, 0)
    step(qi, True)
    for p in range(A_HEADS // 2):
        o2 = jnp.concatenate([(acc_ref[2 * p] / l_ref[2 * p]).astype(BF16),
                              (acc_ref[2 * p + 1] / l_ref[2 * p + 1]).astype(BF16)], axis=0)
        o_a = _dot(wuv_ref[p], o2)
        rows = slice(p * LANE, (p + 1) * LANE)
        y_ref[0, rows, :] = (o_a * _silu(gt_ref[0, rows, :])).astype(y_ref.dtype)


def _mla_prompt(qc, kc, vt, ga_t, w_uvp_t, tq):
    b, h, s, _ = qc.shape
    return pl.pallas_call(
        functools.partial(_mla_kernel, tq=tq),
        grid=(b, s // tq),
        in_specs=[pl.BlockSpec((1, h, tq, QK_WIDTH), lambda bi, qi: (bi, 0, qi, 0)),
                  pl.BlockSpec((1, s, QK_WIDTH), lambda bi, qi: (bi, 0, 0)),
                  pl.BlockSpec((1, KV_LORA, s), lambda bi, qi: (bi, 0, 0)),
                  pl.BlockSpec((1, A_WIDTH, tq), lambda bi, qi: (bi, 0, qi)),
                  _full_spec(w_uvp_t, 2)],
        out_specs=pl.BlockSpec((1, A_WIDTH, tq), lambda bi, qi: (bi, 0, qi)),
        out_shape=jax.ShapeDtypeStruct((b, A_WIDTH, s), BF16),
        scratch_shapes=[pltpu.VMEM((h, 1, tq), F32), pltpu.VMEM((h, 1, tq), F32),
                        pltpu.VMEM((h, KV_LORA, tq), F32)],
        compiler_params=_params(2),
        name="mla_prompt",
    )(qc, kc, vt, ga_t, w_uvp_t)


def _mla_sample_kernel(q_ref, ckv_ref, kpe_ref, kn_ref, g_ref, wuv_ref, y_ref):
    hh, t, _ = q_ref.shape[1:]
    q = q_ref[0].reshape(hh * t, QK_WIDTH)
    ckv_c = ckv_ref[0].astype(BF16)
    kpe_c = kpe_ref[0].astype(BF16)
    kn = kn_ref[0]
    s_c = _dot_t(q[:, 0:KV_LORA], ckv_c) + _dot_t(q[:, KV_LORA:KV_LORA + ROPE_DIM], kpe_c)
    s_n = _dot_t(q, kn)
    m = jnp.maximum(jnp.max(s_c, axis=-1, keepdims=True), jnp.max(s_n, axis=-1, keepdims=True))
    p_c = jnp.exp2(s_c - m)
    p_n = jnp.exp2(s_n - m)
    l = jnp.sum(p_c, axis=-1, keepdims=True) + jnp.sum(p_n, axis=-1, keepdims=True)
    o = ((_dot(p_c.astype(BF16), ckv_c) + _dot(p_n.astype(BF16), kn[:, 0:KV_LORA])) / l).astype(BF16)
    parts = []
    for p in range(hh // 2):
        parts.append(_dot(o[2 * p * t:(2 * p + 1) * t], wuv_ref[2 * p])
                     + _dot(o[(2 * p + 1) * t:(2 * p + 2) * t], wuv_ref[2 * p + 1]))
    o_a = jnp.concatenate(parts, axis=-1)
    y_ref[0] = (o_a * _silu(g_ref[0])).astype(y_ref.dtype)


def _mla_sample(qc, cache_ckv, cache_kpe, kc, ga, w_uvx):
    b, h, t, _ = qc.shape
    past = cache_ckv.shape[1]
    return pl.pallas_call(
        _mla_sample_kernel,
        grid=(b,),
        in_specs=[pl.BlockSpec((1, h, t, QK_WIDTH), lambda bi: (bi, 0, 0, 0)),
                  pl.BlockSpec((1, past, KV_LORA), lambda bi: (bi, 0, 0)),
                  pl.BlockSpec((1, past, ROPE_DIM), lambda bi: (bi, 0, 0)),
                  pl.BlockSpec((1, t, QK_WIDTH), lambda bi: (bi, 0, 0)),
                  pl.BlockSpec((1, t, A_WIDTH), lambda bi: (bi, 0, 0)),
                  _full_spec(w_uvx, 1)],
        out_specs=pl.BlockSpec((1, t, A_WIDTH), lambda bi: (bi, 0, 0)),
        out_shape=jax.ShapeDtypeStruct((b, t, A_WIDTH), BF16),
        compiler_params=_params(1),
        name="mla_sample",
    )(qc, cache_ckv, cache_kpe, kc, ga, w_uvx)


def _pair_rows(q):
    lane = lax.broadcasted_iota(jnp.int32, q.shape, 1)
    zero = jnp.zeros_like(q)
    return jnp.concatenate([jnp.where(lane < HEAD_DIM, q, zero),
                            jnp.where(lane >= HEAD_DIM, q, zero)], axis=0)


def _rel_bias_rows(rb_ref, h, rows):
    base = jnp.broadcast_to(rb_ref[h:h + 1, :], (rows, ROLL_W))
    return pltpu.roll(base, 0, 1, stride=1, stride_axis=0)


def _band_kernel(q_ref, kp_ref, kcur_ref, vtp_ref, vtcur_ref, rb_ref, gt_ref, y_ref,
                 k_scr, vt_scr, bias_ref, *, tq):
    i = pl.program_id(1)

    @pl.when(jnp.logical_and(pl.program_id(0) == 0, i == 0))
    def _():
        a = lax.broadcasted_iota(jnp.int32, (SUB_Q, SUB_W), 0) // CHUNK
        j = lax.broadcasted_iota(jnp.int32, (SUB_Q, SUB_W), 1) // CHUNK
        back = a + BAND_CHUNKS - j
        visible = jnp.logical_and(back >= 0, back <= BAND_CHUNKS)
        for h in range(B_HEADS):
            tile = _rel_bias_rows(rb_ref, h, SUB_Q)[:, 0:SUB_W] * LOG2E
            tile = jnp.where(visible, tile, NEG)
            bias_ref[:, h * SUB_Q:(h + 1) * SUB_Q] = tile.T

    k_scr[:, 0:tq, :] = kp_ref[0]
    k_scr[:, tq:2 * tq, :] = kcur_ref[0]
    vt_scr[:, :, 0:tq] = vtp_ref[0]
    vt_scr[:, :, tq:2 * tq] = vtcur_ref[0]

    def blocks(first):
        pw = 2 * SUB_Q
        units = [(r, g) for r in range(tq // SUB_Q) for g in range(N_PAIR // BAND_GROUP)]

        def scores(u):
            r, g = u
            lo = tq - BAND_KEYS + r * SUB_Q
            parts = []
            for p in range(g * BAND_GROUP, (g + 1) * BAND_GROUP):
                q2 = _pair_rows(q_ref[0, p, r * SUB_Q:(r + 1) * SUB_Q, :])
                parts.append(_dot_t(k_scr[p, lo:lo + SUB_W, :], q2))
            return jnp.concatenate(parts, axis=1)

        s_next = scores(units[0])
        for n, (r, g) in enumerate(units):
            s = s_next + bias_ref[:, g * BAND_GROUP * pw:(g + 1) * BAND_GROUP * pw]
            if n + 1 < len(units):
                s_next = scores(units[n + 1])
            lo = tq - BAND_KEYS + r * SUB_Q
            if first:
                key = lax.broadcasted_iota(jnp.int32, s.shape, 0)
                s = jnp.where(key >= tq - lo, s, NEG)
            m = jnp.max(s, axis=0, keepdims=True)
            e = jnp.exp2(s - m)
            l = jnp.sum(e, axis=0, keepdims=True)
            eb = e.astype(BF16)
            cols = slice(r * SUB_Q, (r + 1) * SUB_Q)
            for k in range(BAND_GROUP):
                p = g * BAND_GROUP + k
                o2 = _dot(vt_scr[p, :, lo:lo + SUB_W], eb[:, k * pw:(k + 1) * pw]) / l[:, k * pw:(k + 1) * pw]
                o = jnp.concatenate([o2[0:HEAD_DIM, 0:SUB_Q], o2[HEAD_DIM:LANE, SUB_Q:pw]], axis=0)
                rows = slice(p * LANE, (p + 1) * LANE)
                y_ref[0, rows, cols] = (o * _silu(gt_ref[0, rows, cols])).astype(y_ref.dtype)

    @pl.when(i == 0)
    def _():
        blocks(True)

    @pl.when(i > 0)
    def _():
        blocks(False)


def _band_prompt(qb, kb, vb_t, rb, gb_t, tq):
    b, np_, s, _ = qb.shape
    n = s // tq
    assert tq >= BAND_KEYS and tq % SUB_Q == 0

    def cur(bi, i):
        return (bi, 0, i, 0)

    def prev(bi, i):
        return (bi, 0, jnp.maximum(i - 1, 0), 0)

    def cur_t(bi, i):
        return (bi, 0, 0, i)

    def prev_t(bi, i):
        return (bi, 0, 0, jnp.maximum(i - 1, 0))

    blk = (1, np_, tq, LANE)
    blk_t = (1, np_, LANE, tq)
    return pl.pallas_call(
        functools.partial(_band_kernel, tq=tq),
        grid=(b, n),
        in_specs=[pl.BlockSpec(blk, cur), pl.BlockSpec(blk, prev), pl.BlockSpec(blk, cur),
                  pl.BlockSpec(blk_t, prev_t), pl.BlockSpec(blk_t, cur_t),
                  _full_spec(rb, 2),
                  pl.BlockSpec((1, B_WIDTH, tq), lambda bi, i: (bi, 0, i))],
        out_specs=pl.BlockSpec((1, B_WIDTH, tq), lambda bi, i: (bi, 0, i)),
        out_shape=jax.ShapeDtypeStruct((b, B_WIDTH, s), BF16),
        scratch_shapes=[pltpu.VMEM((np_, 2 * tq, LANE), BF16), pltpu.VMEM((np_, LANE, 2 * tq), BF16),
                        pltpu.VMEM((SUB_W, B_HEADS * SUB_Q), F32)],
        compiler_params=_params(2),
        name="band_prompt",
    )(qb, kb, kb, vb_t, vb_t, rb, gb_t)


def _band_sample_kernel(q_ref, kn_ref, vn_ref, kc_ref, vc_ref, rb_ref, g_ref, y_ref, bias_ref):
    t = q_ref.shape[2]
    kl = kc_ref.shape[1]

    @pl.when(pl.program_id(0) == 0)
    def _():
        for h in range(B_HEADS):
            bias_ref[h // 2, (h % 2) * t:(h % 2 + 1) * t, :] = _rel_bias_rows(rb_ref, h, t) * LOG2E

    for p in range(N_PAIR):
        sl = slice(p * LANE, (p + 1) * LANE)
        q2 = _pair_rows(q_ref[0, p])
        kc = kc_ref[0, :, sl].astype(BF16)
        vc = vc_ref[0, :, sl].astype(BF16)
        kn = kn_ref[0, p]
        vn = vn_ref[0, p]
        s_c = _dot_t(q2, kc) + bias_ref[p, :, 0:kl]
        s_n = _dot_t(q2, kn) + bias_ref[p, :, kl:kl + t]
        m = jnp.maximum(jnp.max(s_c, axis=-1, keepdims=True), jnp.max(s_n, axis=-1, keepdims=True))
        e_c = jnp.exp2(s_c - m)
        e_n = jnp.exp2(s_n - m)
        l = jnp.sum(e_c, axis=-1, keepdims=True) + jnp.sum(e_n, axis=-1, keepdims=True)
        o2 = (_dot(e_c.astype(BF16), vc) + _dot(e_n.astype(BF16), vn)) / l
        lane = lax.broadcasted_iota(jnp.int32, (t, LANE), 1)
        o = jnp.where(lane < HEAD_DIM, o2[0:t], o2[t:2 * t])
        y_ref[0, :, sl] = (o * _silu(g_ref[0, :, sl])).astype(y_ref.dtype)


def _band_sample(qb, kb, vb, cache_kb, cache_vb, rb, gb):
    b, np_, t, _ = qb.shape
    kl = cache_kb.shape[1]
    assert kl == BAND_KEYS and kl + t <= ROLL_W - LANE
    new = pl.BlockSpec((1, np_, t, LANE), lambda bi: (bi, 0, 0, 0))
    cache = pl.BlockSpec((1, kl, B_WIDTH), lambda bi: (bi, 0, 0))
    return pl.pallas_call(
        _band_sample_kernel,
        grid=(b,),
        in_specs=[new, new, new, cache, cache, _full_spec(rb, 1),
                  pl.BlockSpec((1, t, B_WIDTH), lambda bi: (bi, 0, 0))],
        out_specs=pl.BlockSpec((1, t, B_WIDTH), lambda bi: (bi, 0, 0)),
        out_shape=jax.ShapeDtypeStruct((b, t, B_WIDTH), BF16),
        scratch_shapes=[pltpu.VMEM((np_, 2 * t, ROLL_W), F32)],
        compiler_params=_params(1),
        name="band_sample",
    )(qb, kb, vb, cache_kb, cache_vb, rb, gb)


def _merge_kernel(ya_ref, yb_ref, x_ref, woa_ref, wob_ref, gf_ref, y_ref, *, feature_major):
    mm = _dot_lhs_t if feature_major else _dot
    acc = mm(ya_ref[0], woa_ref[...]) + mm(yb_ref[0], wob_ref[...]) + x_ref[0]
    y_ref[0] = _rms(acc, gf_ref[...])


def _merge(ya, yb, x, w, tm, feature_major):
    b, s, _ = x.shape
    if feature_major:
        act = pl.BlockSpec((1, A_WIDTH, tm), lambda bi, i: (bi, 0, i))
    else:
        act = pl.BlockSpec((1, tm, A_WIDTH), lambda bi, i: (bi, i, 0))
    weights = [w["w_out_a"], w["w_out_b"], w["g_final"]]
    return pl.pallas_call(
        functools.partial(_merge_kernel, feature_major=feature_major),
        grid=(b, s // tm),
        in_specs=[act, act, pl.BlockSpec((1, tm, D_MODEL), lambda bi, i: (bi, i, 0))]
        + [_full_spec(a, 2) for a in weights],
        out_specs=pl.BlockSpec((1, tm, D_MODEL), lambda bi, i: (bi, i, 0)),
        out_shape=jax.ShapeDtypeStruct((b, s, D_MODEL), F32),
        compiler_params=_params(2),
        name="merge",
    )(ya, yb, x, *weights)


def _rope_tables(pos):
    half = ROPE_DIM // 2
    inv = ROPE_BASE ** (-jnp.arange(half, dtype=F32) / half)
    ang = pos.astype(F32)[:, None] * inv
    reps = 2 * ROPE_PER_GROUP
    return jnp.tile(jnp.cos(ang), (1, reps)), jnp.tile(jnp.sin(ang), (1, reps))


def _rot_cols(w):
    half = ROPE_DIM // 2
    return jnp.concatenate([-w[..., half:], w[..., :half]], axis=-1)


def _pad_lanes(w, width):
    return jnp.pad(w, [(0, 0)] * (w.ndim - 1) + [(0, width - w.shape[-1])])


def _block_diag_pairs(m):
    h, r, c = m.shape
    z = jnp.zeros((h // 2, r, c), m.dtype)
    top = jnp.concatenate([m[0::2], z], axis=2)
    bot = jnp.concatenate([z, m[1::2]], axis=2)
    return jnp.concatenate([top, bot], axis=1)


def _prep_weights(w_in, g_mix, g_cq, w_uq, g_ckv, w_uk, w_uv, w_out, g_final):
    w_kr = w_in[:, OFF_KR:OFF_GA]
    w_kv = jnp.concatenate([w_in[:, OFF_CKV:OFF_KR], _pad_lanes(w_kr, LANE),
                            _pad_lanes(_rot_cols(w_kr), LANE)], axis=-1)
    w_pe = w_uq[:, :, NOPE_DIM:]
    w_qpp = jnp.concatenate([w_pe.reshape(Q_LORA, -1), _rot_cols(w_pe).reshape(Q_LORA, -1)], axis=-1)
    uk_t = jnp.transpose(w_uk, (1, 2, 0))
    uv = jnp.transpose(w_uv, (1, 0, 2))
    uv_t = jnp.transpose(w_uv, (1, 2, 0))
    even = (jnp.arange(A_HEADS) % 2 == 0)[:, None, None]
    zeros_uv = jnp.zeros_like(uv)
    w_uvx = jnp.where(even, jnp.concatenate([uv, zeros_uv], axis=2),
                      jnp.concatenate([zeros_uv, uv], axis=2))
    return {
        "g_mix": g_mix.reshape(1, D_MODEL),
        "w_cq": w_in[:, OFF_CQ:OFF_CKV].astype(BF16),
        "w_kv": w_kv.astype(BF16),
        "w_ga": w_in[:, OFF_GA:OFF_QB].astype(BF16),
        "w_qb": w_in[:, OFF_QB:OFF_KB].astype(BF16),
        "w_kb": w_in[:, OFF_KB:OFF_VB].astype(BF16),
        "w_vb": w_in[:, OFF_VB:OFF_GB].astype(BF16),
        "w_gb": w_in[:, OFF_GB:].astype(BF16),
        "g_cq": g_cq.reshape(1, Q_LORA),
        "w_qn": w_uq[:, :, :NOPE_DIM].reshape(Q_LORA, A_WIDTH).astype(BF16),
        "w_qpp": w_qpp.astype(BF16),
        "g_ckv": g_ckv.reshape(1, KV_LORA),
        "w_ukp": _block_diag_pairs(uk_t).astype(BF16),
        "w_uvp_t": _block_diag_pairs(uv_t).astype(BF16),
        "w_uvx": w_uvx.astype(BF16),
        "w_out_a": w_out[:A_WIDTH].astype(BF16),
        "w_out_b": w_out[A_WIDTH:].astype(BF16),
        "g_final": g_final.reshape(1, D_MODEL),
    }


def _rel_bias_vector(rel_bias):
    far = rel_bias[:, 2 * MAX_REL:]
    n_far = BAND_KEYS - MAX_REL + 1
    near = rel_bias[:, 2 * MAX_REL - 1:0:-1]
    n_tail = ROLL_W - n_far - near.shape[1]
    h = rel_bias.shape[0]
    return jnp.concatenate([jnp.broadcast_to(far, (h, n_far)), near,
                            jnp.broadcast_to(far, (h, n_tail))], axis=1).astype(F32)


def _layer(xp, xs, c_ckv, c_kpe, c_kb, c_vb, w, rel_bias, tabs_p, tabs_s):
    b, s, _ = xp.shape
    bs, t, _ = xs.shape
    kl = c_kb.shape[1]
    tm = PROMPT_ROWS
    rb = _rel_bias_vector(rel_bias)

    qc, kc, ckv, kpe, qb, kb, kbt, vbt, ga_t, gb_t, vb_t, vt = _proj(xp, *tabs_p, w, tm, True)
    ya_t = _mla_prompt(qc, kc, vt, ga_t, w["w_uvp_t"], tm)
    yb_t = _band_prompt(qb, kb, vb_t, rb, gb_t, tm)
    y_p = _merge(ya_t, yb_t, xp, w, tm, True)

    qc_s, kc_s, ckv_s, kpe_s, qb_s, kb_s, kbt_s, vbt_s, ga_s, gb_s, vb_s = _proj(xs, *tabs_s, w, t, False)
    ya_s = _mla_sample(qc_s, c_ckv, c_kpe, kc_s, ga_s, w["w_uvx"])
    yb_s = _band_sample(qb_s, kb_s, vb_s, c_kb.reshape(bs, kl, B_WIDTH), c_vb.reshape(bs, kl, B_WIDTH),
                        rb, gb_s)
    y_s = _merge(ya_s, yb_s, xs, w, t, False)

    kt = kbt.shape[1]
    outs_p = (ckv, kpe, kbt.reshape(b, kt, B_HEADS, HEAD_DIM), vbt.reshape(b, kt, B_HEADS, HEAD_DIM))
    outs_s = (ckv_s, kpe_s, kbt_s.reshape(bs, t, B_HEADS, HEAD_DIM), vbt_s.reshape(bs, t, B_HEADS, HEAD_DIM))
    return y_p, y_s, outs_p, outs_s


def kernel(x_prompt, x_sample, cache_ckv, cache_kpe, cache_kb, cache_vb, w_in, g_mix, g_cq, w_uq,
           g_ckv, w_uk, w_uv, rel_bias, w_out, g_final):
    depth = w_in.shape[0]
    assert depth == 1, "final norm is fused into the single layer's merge kernel"
    s = x_prompt.shape[1]
    t = x_sample.shape[1]
    past = cache_ckv.shape[2]
    tabs_p = _rope_tables(jnp.arange(s))
    tabs_s = _rope_tables(past + jnp.arange(t))
    w = _prep_weights(w_in[0], g_mix[0], g_cq[0], w_uq[0], g_ckv[0], w_uk[0], w_uv[0], w_out[0], g_final)
    y_p, y_s, outs_p, outs_s = _layer(x_prompt, x_sample, cache_ckv[0], cache_kpe[0], cache_kb[0],
                                      cache_vb[0], w, rel_bias[0], tabs_p, tabs_s)
    return (y_p, y_s) + tuple(o[None] for o in outs_p) + tuple(o[None] for o in outs_s)
```
